```python
import jax, jax.numpy as jnp
from jax import lax
import numpy as np

D_MODEL = 2048
BATCH = 4
SEQ = 8192
DEPTH = 1

LRU_WIDTH = 1024
LRU_BLOCKS = 8
LRU_BLOCK_DIM = LRU_WIDTH // LRU_BLOCKS
CONV_WIDTH = 4
LRU_C = 8.0
ATTN_HEADS = 8
HEAD_DIM = 128
ATTN_WIDTH = ATTN_HEADS * HEAD_DIM
MOBA_BLOCK = 256
MOBA_TOPK = 3
QUERY_CHUNK = 32
N_GROUPS = 4
EXPERTS_PER_GROUP = 8
N_EXPERTS = N_GROUPS * EXPERTS_PER_GROUP
TOPK_IN_GROUP = 2
EXPERT_HIDDEN = 512
RMS_EPS = 1e-6

IN_WIDTH = 2 * LRU_WIDTH + 3 * ATTN_WIDTH + 2 * D_MODEL
SPLIT_POINTS = (LRU_WIDTH, 2 * LRU_WIDTH, 2 * LRU_WIDTH + ATTN_WIDTH, 2 * LRU_WIDTH + 2 * ATTN_WIDTH, 2 * LRU_WIDTH + 3 * ATTN_WIDTH, 2 * LRU_WIDTH + 3 * ATTN_WIDTH + D_MODEL)

kernel_name = "hybrid_rglru_moba_hmoe_block"


def rms_norm(x, g):
    xf = x.astype(jnp.float32)
    y = xf * lax.rsqrt(jnp.mean(xf * xf, axis=-1, keepdims=True) + RMS_EPS)
    return (y * g.astype(jnp.float32)).astype(x.dtype)


def causal_conv(x, w, b):
    S = x.shape[1]
    xp = jnp.pad(x, ((0, 0), (CONV_WIDTH - 1, 0), (0, 0)))
    y = b
    for k in range(CONV_WIDTH):
        y = y + xp[:, k:k + S] * w[k]
    return y


def _lin_rec_combine(c1, c2):
    a1, b1 = c1
    a2, b2 = c2
    return a1 * a2, a2 * b1 + b2


def rglru(x, wa, ba, wx, bx, lam):
    B, S, W = x.shape
    xb = x.reshape(B, S, LRU_BLOCKS, LRU_BLOCK_DIM)
    r = jax.nn.sigmoid(jnp.einsum('bsnc,ncd->bsnd', xb, wa).reshape(B, S, W) + ba)
    i = jax.nn.sigmoid(jnp.einsum('bsnc,ncd->bsnd', xb, wx).reshape(B, S, W) + bx)
    log_a = LRU_C * r.astype(jnp.float32) * jax.nn.log_sigmoid(lam.astype(jnp.float32))
    a = jnp.exp(log_a)
    b_in = jnp.sqrt(-jnp.expm1(2.0 * log_a)) * (i * x).astype(jnp.float32)
    _, h = lax.associative_scan(_lin_rec_combine, (a, b_in), axis=1)
    return h.astype(x.dtype)


def alibi_slopes(n_heads):
    return 2.0 ** (-8.0 * jnp.arange(1, n_heads + 1, dtype=jnp.float32) / n_heads)


def moba_attention(q, k, v):
    B, S, H, dh = q.shape
    nb = -(-S // MOBA_BLOCK)
    s_pad = nb * MOBA_BLOCK
    pad = ((0, 0), (0, s_pad - S), (0, 0), (0, 0))
    q, k, v = [jnp.pad(t, pad).transpose(0, 2, 1, 3) for t in (q, k, v)]
    k_blocks = k.reshape(B, H, nb, MOBA_BLOCK, dh)
    v_blocks = v.reshape(B, H, nb, MOBA_BLOCK, dh)
    k_mean = jnp.mean(k_blocks.astype(jnp.float32), axis=3)
    n_sel = min(MOBA_TOPK, nb)
    slopes = alibi_slopes(H)
    scale = HEAD_DIM ** -0.5
    b_idx = jnp.arange(B)[:, None, None, None]
    h_idx = jnp.arange(H)[None, :, None, None]
    blk_pos = jnp.arange(MOBA_BLOCK)
    f32 = jnp.float32

    def chunk(c):
        start = c * QUERY_CHUNK
        j = start // MOBA_BLOCK
        q_c = lax.dynamic_slice_in_dim(q, start, QUERY_CHUNK, axis=2)
        t = start + jnp.arange(QUERY_CHUNK)
        gate = jnp.einsum('bhqd,bhnd->bhqn', q_c.astype(f32), k_mean)
        gate = jnp.where(jnp.arange(nb) < j, gate, -jnp.inf)
        _, idx = lax.top_k(gate, n_sel)
        valid = idx < j
        k_g = k_blocks[b_idx, h_idx, idx]
        v_g = v_blocks[b_idx, h_idx, idx]
        s_sel = jnp.einsum('bhqd,bhqnkd->bhqnk', q_c, k_g, preferred_element_type=f32) * scale
        dist_sel = (t[:, None, None] - (idx[..., None] * MOBA_BLOCK + blk_pos)).astype(f32)
        s_sel = s_sel - slopes[:, None, None, None] * dist_sel
        s_sel = jnp.where(valid[..., None], s_sel, -jnp.inf)
        k_own = lax.dynamic_index_in_dim(k_blocks, j, axis=2, keepdims=False)
        v_own = lax.dynamic_index_in_dim(v_blocks, j, axis=2, keepdims=False)
        s_own = jnp.einsum('bhqd,bhkd->bhqk', q_c, k_own, preferred_element_type=f32) * scale
        dist_own = (t[:, None] - (j * MOBA_BLOCK + blk_pos)).astype(f32)
        s_own = jnp.where(dist_own >= 0, s_own - slopes[:, None, None] * dist_own, -jnp.inf)
        scores = jnp.concatenate([s_sel.reshape(B, H, QUERY_CHUNK, n_sel * MOBA_BLOCK), s_own], axis=-1)
        p = jax.nn.softmax(scores, axis=-1)
        p_sel = p[..., :n_sel * MOBA_BLOCK].reshape(B, H, QUERY_CHUNK, n_sel, MOBA_BLOCK).astype(v.dtype)
        p_own = p[..., n_sel * MOBA_BLOCK:].astype(v.dtype)
        out = (jnp.einsum('bhqnk,bhqnkd->bhqd', p_sel, v_g, preferred_element_type=f32)
               + jnp.einsum('bhqk,bhkd->bhqd', p_own, v_own, preferred_element_type=f32))
        return out.astype(q.dtype)

    out = lax.map(chunk, jnp.arange(s_pad // QUERY_CHUNK))
    out = out.transpose(1, 0, 3, 2, 4).reshape(B, s_pad, H * dh)
    return out[:, :S]


def hier_moe(h, wg_r, bg_r, we_r, be_r, w_gate, w_up, w_down):
    B, S, D = h.shape
    tok = h.reshape(B * S, D)
    f32 = jnp.float32
    grp_p = jax.nn.softmax((tok @ wg_r).astype(f32) + bg_r.astype(f32), axis=-1)
    grp_w, grp_i = lax.top_k(grp_p, 1)
    exp_logits = ((tok @ we_r).astype(f32) + be_r.astype(f32)).reshape(-1, N_GROUPS, EXPERTS_PER_GROUP)
    in_grp = jnp.take_along_axis(exp_logits, grp_i[:, :, None], axis=1)[:, 0]
    ew, ei = lax.top_k(jax.nn.softmax(in_grp, axis=-1), TOPK_IN_GROUP)
    ew = ew / jnp.sum(ew, axis=-1, keepdims=True)
    weights = grp_w * ew
    global_i = grp_i * EXPERTS_PER_GROUP + ei
    combine = jnp.sum(jax.nn.one_hot(global_i, N_EXPERTS, dtype=f32) * weights[..., None], axis=1)

    def body(acc, xs):
        wg, wu, wd, c = xs
        y = (jax.nn.silu(tok @ wg) * (tok @ wu)) @ wd
        return acc + c[:, None] * y.astype(f32), None

    acc, _ = lax.scan(body, jnp.zeros((B * S, D), f32), (w_gate, w_up, w_down, combine.T))
    return acc.astype(h.dtype).reshape(B, S, D)


def setup_inputs(seed: int = 0) -> dict:
    key = jax.random.key(seed)
    ks = jax.random.split(key, 24)
    L, D = DEPTH, D_MODEL
    nrm = lambda k, shape, s: jax.random.normal(k, shape, jnp.float32) * s
    u = jax.random.uniform(ks[9], (L, LRU_WIDTH), jnp.float32, 0.9, 0.999)
    a0 = u ** (1.0 / LRU_C)
    lam = jnp.log(a0) - jnp.log1p(-a0)
    return {
        "x": nrm(ks[0], (BATCH, SEQ, D), 1.0),
        "norm_attn_g": 1.0 + nrm(ks[1], (L, D), 0.02),
        "w_in": nrm(ks[2], (L, D, IN_WIDTH), D ** -0.5),
        "conv_w": nrm(ks[3], (L, CONV_WIDTH, LRU_WIDTH), CONV_WIDTH ** -0.5),
        "conv_b": nrm(ks[4], (L, LRU_WIDTH), 0.01),
        "lru_wa": nrm(ks[5], (L, LRU_BLOCKS, LRU_BLOCK_DIM, LRU_BLOCK_DIM), LRU_BLOCK_DIM ** -0.5),
        "lru_ba": nrm(ks[6], (L, LRU_WIDTH), 0.01),
        "lru_wx": nrm(ks[7], (L, LRU_BLOCKS, LRU_BLOCK_DIM, LRU_BLOCK_DIM), LRU_BLOCK_DIM ** -0.5),
        "lru_bx": nrm(ks[8], (L, LRU_WIDTH), 0.01),
        "lru_lambda": lam,
        "proj_rec": nrm(ks[10], (L, LRU_WIDTH, D), LRU_WIDTH ** -0.5),
        "proj_attn": nrm(ks[11], (L, ATTN_WIDTH, D), ATTN_WIDTH ** -0.5),
        "w_out": nrm(ks[12], (L, D, D), D ** -0.5),
        "norm_ffn_g": 1.0 + nrm(ks[13], (L, D), 0.02),
        "router_group_w": nrm(ks[14], (L, D, N_GROUPS), D ** -0.5),
        "router_group_b": nrm(ks[15], (L, N_GROUPS), 0.01),
        "router_expert_w": nrm(ks[16], (L, D, N_EXPERTS), D ** -0.5),
        "router_expert_b": nrm(ks[17], (L, N_EXPERTS), 0.01),
        "expert_w_gate": nrm(ks[18], (L, N_EXPERTS, D, EXPERT_HIDDEN), D ** -0.5),
        "expert_w_up": nrm(ks[19], (L, N_EXPERTS, D, EXPERT_HIDDEN), D ** -0.5),
        "expert_w_down": nrm(ks[20], (L, N_EXPERTS, EXPERT_HIDDEN, D), EXPERT_HIDDEN ** -0.5),
        "norm_final_g": 1.0 + nrm(ks[21], (D,), 0.02),
    }


def reference(x, norm_attn_g, w_in, conv_w, conv_b, lru_wa, lru_ba, lru_wx, lru_bx, lru_lambda, proj_rec, proj_attn, w_out, norm_ffn_g, router_group_w, router_group_b, router_expert_w, router_expert_b, expert_w_gate, expert_w_up, expert_w_down, norm_final_g):
    B, S, _ = x.shape
    for l in range(DEPTH):
        h = rms_norm(x, norm_attn_g[l])
        u = h @ w_in[l]
        x_rec, g_rec, q, k, v, gate_rec, gate_attn = jnp.split(u, SPLIT_POINTS, axis=-1)
        y_rec = rglru(causal_conv(x_rec, conv_w[l], conv_b[l]), lru_wa[l], lru_ba[l], lru_wx[l], lru_bx[l], lru_lambda[l])
        y_rec = y_rec * jax.nn.gelu(g_rec, approximate=True)
        y_attn = moba_attention(q.reshape(B, S, ATTN_HEADS, HEAD_DIM), k.reshape(B, S, ATTN_HEADS, HEAD_DIM), v.reshape(B, S, ATTN_HEADS, HEAD_DIM))
        merged = jax.nn.sigmoid(gate_rec) * (y_rec @ proj_rec[l]) + jax.nn.sigmoid(gate_attn) * (y_attn @ proj_attn[l])
        x = x + merged @ w_out[l]
        x = x + hier_moe(rms_norm(x, norm_ffn_g[l]), router_group_w[l], router_group_b[l], router_expert_w[l], router_expert_b[l], expert_w_gate[l], expert_w_up[l], expert_w_down[l])
    return rms_norm(x, norm_final_g)
```

```python
import functools

import jax
import jax.numpy as jnp
from jax import lax
from jax.experimental import pallas as pl
from jax.experimental.pallas import tpu as pltpu

F32 = jnp.float32
BF16 = jnp.bfloat16

RMS_EPS = 1e-6
LRU_BLOCKS = 8
CONV_WIDTH = 4
LRU_C = 8.0
ATTN_HEADS = 8
HEAD_DIM = 128
MOBA_BLOCK = 256
MOBA_TOPK = 3
N_GROUPS = 4
EXPERTS_PER_GROUP = 8
N_EXPERTS = N_GROUPS * EXPERTS_PER_GROUP

LANES = 128
SUBLANES = 8
NEG_BIG = -1e30
ROW_TILE = 256
VMEM_LIMIT = 56 * 1024 * 1024


def _cparams(sem):
    return pltpu.CompilerParams(dimension_semantics=sem, vmem_limit_bytes=VMEM_LIMIT)


def _const_spec(shape):
    nd = len(shape)
    return pl.BlockSpec(shape, lambda *_: (0,) * nd, pipeline_mode=pl.Buffered(1))


def _inproj_kernel(x_ref, g_ref, w_ref, o_ref, h_ref, *, chunk):
    tm = x_ref.shape[0]

    @pl.when(pl.program_id(1) == 0)
    def _():
        def body(c, _):
            r0 = pl.multiple_of(c * chunk, chunk)
            x = x_ref[pl.ds(r0, chunk), :]
            ms = jnp.mean(x * x, axis=-1, keepdims=True)
            h_ref[pl.ds(r0, chunk), :] = (x * lax.rsqrt(ms + RMS_EPS) * g_ref[...]).astype(BF16)
            return 0
        lax.fori_loop(0, tm // chunk, body, 0)

    o_ref[...] = jnp.dot(h_ref[...], w_ref[...], preferred_element_type=F32).astype(o_ref.dtype)


def _inproj(x2, g, w_bf16, tm=1024, tn=1024):
    T, D = x2.shape
    N = w_bf16.shape[1]
    tm = min(tm, T)
    return pl.pallas_call(
        functools.partial(_inproj_kernel, chunk=min(256, tm)),
        grid=(T // tm, N // tn),
        in_specs=[
            pl.BlockSpec((tm, D), lambda i, j: (i, 0)),
            pl.BlockSpec((1, D), lambda i, j: (0, 0)),
            pl.BlockSpec((D, tn), lambda i, j: (0, j)),
        ],
        out_specs=pl.BlockSpec((tm, tn), lambda i, j: (i, j)),
        out_shape=jax.ShapeDtypeStruct((T, N), BF16),
        scratch_shapes=[pltpu.VMEM((tm, D), BF16)],
        compiler_params=_cparams(("parallel", "arbitrary")),
        name="inproj",
    )(x2, g, w_bf16)


def _expm1(y, u):
    near = (u - 1.0) * y / jnp.log(u)
    return jnp.where(u == 1.0, y, jnp.where(y > -0.5, near, u - 1.0))


def _rglru_kernel(xr_ref, gr_ref, cw_ref, cb_ref, wa_ref, ba_ref, wx_ref, bx_ref, lam_ref,
                  o_ref, xbuf_ref, hc_ref):
    ts = xr_ref.shape[1]
    W = xr_ref.shape[2]
    bd = W // LRU_BLOCKS
    pad = SUBLANES

    @pl.when(pl.program_id(1) == 0)
    def _():
        xbuf_ref[0:pad, :] = jnp.zeros((pad, W), F32)
        hc_ref[...] = jnp.zeros_like(hc_ref)

    xbuf_ref[pad:pad + ts, :] = xr_ref[0].astype(F32)
    xc = cb_ref[...] + cw_ref[0:1, :] * xbuf_ref[pl.ds(pad - 3, ts), :]
    for k in range(1, CONV_WIDTH):
        xc = xc + cw_ref[k:k + 1, :] * xbuf_ref[pl.ds(pad - 3 + k, ts), :]
    xbuf_ref[0:pad, :] = xbuf_ref[ts:ts + pad, :]

    xcb = xc.astype(BF16)
    ra, ri = [], []
    for n in range(LRU_BLOCKS):
        blk = xcb[:, n * bd:(n + 1) * bd]
        ra.append(jnp.dot(blk, wa_ref[n], preferred_element_type=F32))
        ri.append(jnp.dot(blk, wx_ref[n], preferred_element_type=F32))
    r = jax.nn.sigmoid(jnp.concatenate(ra, axis=-1) + ba_ref[...])
    i = jax.nn.sigmoid(jnp.concatenate(ri, axis=-1) + bx_ref[...])
    lam = lam_ref[...]
    log_sig = jnp.minimum(lam, 0.0) - jnp.log1p(jnp.exp(-jnp.abs(lam)))
    log_a = LRU_C * r * log_sig
    a = jnp.exp(log_a)
    b = jnp.sqrt(-_expm1(2.0 * log_a, a * a)) * (i * xc)

    row = lax.broadcasted_iota(jnp.int32, (ts, W), 0)
    d = 1
    while d < ts:
        keep = row >= d
        a_sh = jnp.where(keep, pltpu.roll(a, d, 0), 1.0)
        b_sh = jnp.where(keep, pltpu.roll(b, d, 0), 0.0)
        b = a * b_sh + b
        a = a * a_sh
        d *= 2
    h = b + a * hc_ref[...]
    hc_ref[...] = h[ts - 1:ts, :]

    o_ref[0] = (h * jax.nn.gelu(gr_ref[0].astype(F32), approximate=True)).astype(o_ref.dtype)


def _rglru(u3, conv_w, conv_b, wa, ba, wx, bx, lam, W, ts=256):
    B, S, _ = u3.shape
    ts = min(ts, S)
    vec = lambda: pl.BlockSpec((1, W), lambda b, s: (0, 0))
    bd = W // LRU_BLOCKS
    return pl.pallas_call(
        _rglru_kernel,
        grid=(B, S // ts),
        in_specs=[
            pl.BlockSpec((1, ts, W), lambda b, s: (b, s, 0)),
            pl.BlockSpec((1, ts, W), lambda b, s: (b, s, 1)),
            pl.BlockSpec((CONV_WIDTH, W), lambda b, s: (0, 0)),
            vec(),
            pl.BlockSpec((LRU_BLOCKS, bd, bd), lambda b, s: (0, 0, 0)),
            vec(),
            pl.BlockSpec((LRU_BLOCKS, bd, bd), lambda b, s: (0, 0, 0)),
            vec(),
            vec(),
        ],
        out_specs=pl.BlockSpec((1, ts, W), lambda b, s: (b, s, 0)),
        out_shape=jax.ShapeDtypeStruct((B, S, W), BF16),
        scratch_shapes=[pltpu.VMEM((SUBLANES + ts, W), F32), pltpu.VMEM((1, W), F32)],
        compiler_params=_cparams(("parallel", "arbitrary")),
        name="rglru",
    )(u3, u3, conv_w, conv_b, wa, ba, wx, bx, lam)


def _moba_kernel(slope_ref, q_ref, k_ref, v_ref, o_ref,
                 vt_ref, kmean_ref, bias_ref, own_ref, sel_ref, m_ref, l_ref, acc_ref):
    BLK = MOBA_BLOCK
    nb = k_ref.shape[1] // BLK
    h = pl.program_id(1)
    j = pl.program_id(2)
    slope = slope_ref[h]

    @pl.when(j == 0)
    def _():
        for n in range(nb):
            kb = k_ref[0, n * BLK:(n + 1) * BLK, :].astype(F32)
            kmean_ref[n:n + 1, :] = jnp.mean(kb, axis=0, keepdims=True)
            vt_ref[n] = v_ref[0, n * BLK:(n + 1) * BLK, :].astype(F32).T.astype(BF16)
        kk = lax.broadcasted_iota(jnp.int32, (BLK, BLK), 0)
        qq = lax.broadcasted_iota(jnp.int32, (BLK, BLK), 1)
        ab = slope * kk.astype(F32)
        bias_ref[...] = ab
        own_ref[...] = jnp.where(kk <= qq, ab, NEG_BIG)

    q = q_ref[0]
    qs = (q.astype(F32) * (HEAD_DIM ** -0.5)).astype(BF16)
    nt = (((1,), (1,)), ((), ()))

    gate = lax.dot_general(kmean_ref[...], q.astype(F32), nt, preferred_element_type=F32,
                           precision=lax.Precision.HIGHEST)
    blk_i = lax.broadcasted_iota(jnp.int32, (nb, BLK), 0)
    past = blk_i < j
    g = jnp.where(past, gate, -jnp.inf)
    sel = jnp.zeros((nb, BLK), jnp.bool_)
    for _ in range(min(MOBA_TOPK, nb)):
        cmax = jnp.max(g, axis=0, keepdims=True)
        first = jnp.min(jnp.where(g == cmax, blk_i, nb), axis=0, keepdims=True)
        hit = blk_i == first
        sel = jnp.logical_or(sel, hit)
        g = jnp.where(hit, -jnp.inf, g)
    selb = jnp.where(jnp.logical_and(sel, past), 0.0, NEG_BIG)
    for n in range(nb):
        sel_ref[n] = jnp.broadcast_to(selb[n:n + 1, :], (SUBLANES, BLK))

    k_own = k_ref[0, pl.ds(pl.multiple_of(j * BLK, BLK), BLK), :]
    z = lax.dot_general(k_own, qs, nt, preferred_element_type=F32) + own_ref[...]
    m0 = jnp.max(z, axis=0, keepdims=True)
    p = jnp.exp(z - m0)
    m_ref[...] = m0
    l_ref[...] = jnp.sum(p, axis=0, keepdims=True)
    acc_ref[...] = jnp.dot(vt_ref[j], p.astype(BF16), preferred_element_type=F32)

    def body(n, _):
        k_n = k_ref[0, pl.ds(pl.multiple_of(n * BLK, BLK), BLK), :]
        dist = jnp.full((SUBLANES, BLK), j - n, jnp.int32).astype(F32)
        rowterm = sel_ref[n] - (slope * BLK) * dist
        z = lax.dot_general(k_n, qs, nt, preferred_element_type=F32) + bias_ref[...]
        z = (z.reshape(BLK // SUBLANES, SUBLANES, BLK) + rowterm[None]).reshape(BLK, BLK)
        m_old = m_ref[...]
        m_new = jnp.maximum(m_old, jnp.max(z, axis=0, keepdims=True))
        alpha = jnp.exp(m_old - m_new)
        p = jnp.exp(z - m_new)
        m_ref[...] = m_new
        l_ref[...] = alpha * l_ref[...] + jnp.sum(p, axis=0, keepdims=True)
        acc_ref[...] = alpha * acc_ref[...] + jnp.dot(vt_ref[n], p.astype(BF16),
                                                      preferred_element_type=F32)
        return 0

    lax.fori_loop(0, j, body, 0)
    o_ref[0] = (acc_ref[...] / l_ref[...]).T.astype(o_ref.dtype)


def _moba(u3, slopes, q_col, k_col, v_col):
    B, S, _ = u3.shape
    BLK, dh, H = MOBA_BLOCK, HEAD_DIM, ATTN_HEADS
    nb = S // BLK
    return pl.pallas_call(
        _moba_kernel,
        grid=(B, H, nb),
        in_specs=[
            pl.BlockSpec(memory_space=pltpu.SMEM),
            pl.BlockSpec((1, BLK, dh), lambda b, h, j: (b, j, q_col + h)),
            pl.BlockSpec((1, S, dh), lambda b, h, j: (b, 0, k_col + h)),
            pl.BlockSpec((1, S, dh), lambda b, h, j: (b, 0, v_col + h)),
        ],
        out_specs=pl.BlockSpec((1, BLK, dh), lambda b, h, j: (b, j, h)),
        out_shape=jax.ShapeDtypeStruct((B, S, H * dh), BF16),
        scratch_shapes=[
            pltpu.VMEM((nb, dh, BLK), BF16),
            pltpu.VMEM((nb, dh), F32),
            pltpu.VMEM((BLK, BLK), F32),
            pltpu.VMEM((BLK, BLK), F32),
            pltpu.VMEM((nb, SUBLANES, BLK), F32),
            pltpu.VMEM((1, BLK), F32),
            pltpu.VMEM((1, BLK), F32),
            pltpu.VMEM((dh, BLK), F32),
        ],
        compiler_params=_cparams(("parallel", "parallel", "arbitrary")),
        name="moba",
    )(slopes, u3, u3, u3)


def _merge_kernel(yr_ref, ya_ref, gr0_ref, gr1_ref, ga0_ref, ga1_ref, x_ref,
                  pr_ref, pa_ref, wo_ref, g_ref, wr_ref, br_ref,
                  x1_ref, h2_ref, lg_ref):
    half = gr0_ref.shape[1]
    pr = jnp.dot(yr_ref[...], pr_ref[...], preferred_element_type=F32)
    pa = jnp.dot(ya_ref[...], pa_ref[...], preferred_element_type=F32)
    sig = lambda ref: jax.nn.sigmoid(ref[...].astype(F32))
    m0 = sig(gr0_ref) * pr[:, :half] + sig(ga0_ref) * pa[:, :half]
    m1 = sig(gr1_ref) * pr[:, half:] + sig(ga1_ref) * pa[:, half:]
    merged = jnp.concatenate([m0, m1], axis=-1).astype(BF16)
    x1 = x_ref[...] + jnp.dot(merged, wo_ref[...], preferred_element_type=F32)
    x1_ref[...] = x1
    ms = jnp.mean(x1 * x1, axis=-1, keepdims=True)
    h2 = x1 * lax.rsqrt(ms + RMS_EPS) * g_ref[...]
    h2_ref[...] = h2
    lg_ref[...] = jnp.dot(h2, wr_ref[...], preferred_element_type=F32,
                          precision=lax.Precision.HIGHEST) + br_ref[...]


def _merge(y_rec, y_attn, u, gate_col, x2, pr, pa, wo, g, wr, br, tm=256):
    T, D = x2.shape
    W = y_rec.shape[1]
    half = D // 2
    tm = min(tm, T)
    gspec = lambda c: pl.BlockSpec((tm, half), lambda i: (i, gate_col + c))
    row = lambda n: pl.BlockSpec((tm, n), lambda i: (i, 0))
    return pl.pallas_call(
        _merge_kernel,
        grid=(T // tm,),
        in_specs=[
            row(W), row(W), gspec(0), gspec(1), gspec(2), gspec(3), row(D),
            _const_spec(pr.shape), _const_spec(pa.shape), _const_spec(wo.shape),
            _const_spec(g.shape), _const_spec(wr.shape), _const_spec(br.shape),
        ],
        out_specs=[row(D), row(D), row(LANES)],
        out_shape=[
            jax.ShapeDtypeStruct((T, D), F32),
            jax.ShapeDtypeStruct((T, D), F32),
            jax.ShapeDtypeStruct((T, LANES), F32),
        ],
        compiler_params=_cparams(("parallel",)),
        name="merge",
    )(y_rec, y_attn, u, u, u, u, x2, pr, pa, wo, g, wr, br)


def _route_kernel(lg_ref, o_ref):
    lg = lg_ref[...]
    G, EPG = N_GROUPS, EXPERTS_PER_GROUP
    lane = lax.broadcasted_iota(jnp.int32, lg.shape, 1)

    def first_argmax(vals, mask):
        v = jnp.where(mask, vals, -jnp.inf)
        mx = jnp.max(v, axis=-1, keepdims=True)
        idx = jnp.min(jnp.where(jnp.logical_and(mask, v == mx), lane, LANES), axis=-1, keepdims=True)
        return mx, idx

    gmask = lane < G
    gmax, gi = first_argmax(lg, gmask)
    gexp = jnp.where(gmask, jnp.exp(lg - gmax), 0.0)
    grp_w = 1.0 / jnp.sum(gexp, axis=-1, keepdims=True)
    lo = G + gi * EPG
    emask = jnp.logical_and(lane >= lo, lane < lo + EPG)
    emax, i1 = first_argmax(lg, emask)
    eexp = jnp.where(emask, jnp.exp(lg - emax), 0.0)
    esum = jnp.sum(eexp, axis=-1, keepdims=True)
    p1 = 1.0 / esum
    mask2 = jnp.logical_and(emask, lane != i1)
    l2, i2 = first_argmax(lg, mask2)
    p2 = jnp.exp(l2 - emax) / esum
    den = p1 + p2
    w1 = grp_w * (p1 / den)
    w2 = grp_w * (p2 / den)
    e1 = (i1 - G).astype(F32)
    e2 = (i2 - G).astype(F32)
    out = jnp.where(lane == 0, e1, jnp.where(lane == 1, e2, jnp.where(lane == 2, w1, jnp.where(lane == 3, w2, 0.0))))
    o_ref[...] = out


def _route(logits, tm=512):
    T = logits.shape[0]
    tm = min(tm, T)
    spec = pl.BlockSpec((tm, LANES), lambda i: (i, 0))
    return pl.pallas_call(
        _route_kernel,
        grid=(T // tm,),
        in_specs=[spec],
        out_specs=spec,
        out_shape=jax.ShapeDtypeStruct((T, LANES), F32),
        compiler_params=_cparams(("parallel",)),
        name="route",
    )(logits)


def _row_copy(src_hbm, src_row, dst_ref, dst_row, sem):
    return pltpu.make_async_copy(src_hbm.at[pl.ds(src_row, 1), :], dst_ref.at[pl.ds(dst_row, 1), :], sem)


def _dispatch_kernel(idx_ref, h_hbm, o_ref, sem):
    n = o_ref.shape[0]

    def start(r, _):
        _row_copy(h_hbm, idx_ref[0, 0, r], o_ref, r, sem).start()
        return 0

    def wait(r, _):
        _row_copy(h_hbm, 0, o_ref, r, sem).wait()
        return 0

    lax.fori_loop(0, n, start, 0)
    lax.fori_loop(0, n, wait, 0)


def _dispatch(h2, src_tok, n_tiles):
    T, D = h2.shape
    return pl.pallas_call(
        _dispatch_kernel,
        grid=(n_tiles,),
        in_specs=[
            pl.BlockSpec((1, 1, ROW_TILE), lambda i: (i, 0, 0), memory_space=pltpu.SMEM),
            pl.BlockSpec(memory_space=pl.ANY),
        ],
        out_specs=pl.BlockSpec((ROW_TILE, D), lambda i: (i, 0)),
        out_shape=jax.ShapeDtypeStruct((n_tiles * ROW_TILE, D), F32),
        scratch_shapes=[pltpu.SemaphoreType.DMA(())],
        compiler_params=_cparams(("arbitrary",)),
        name="dispatch",
    )(src_tok.reshape(n_tiles, 1, ROW_TILE), h2)


def _ffn_kernel(te_ref, tv_ref, x_ref, wg_ref, wu_ref, wd_ref, o_ref):
    i = pl.program_id(0)

    @pl.when(tv_ref[i] != 0)
    def _():
        x = x_ref[...].astype(BF16)
        g = jnp.dot(x, wg_ref[0], preferred_element_type=F32)
        u = jnp.dot(x, wu_ref[0], preferred_element_type=F32)
        hmid = (jax.nn.silu(g) * u).astype(BF16)
        o_ref[...] = jnp.dot(hmid, wd_ref[0], preferred_element_type=F32)

    @pl.when(tv_ref[i] == 0)
    def _():
        o_ref[...] = jnp.zeros_like(o_ref)


def _expert_ffn(xs, tile_e, tile_valid, wg, wu, wd):
    R, D = xs.shape
    Hd = wg.shape[2]
    n_tiles = R // ROW_TILE
    grid_spec = pltpu.PrefetchScalarGridSpec(
        num_scalar_prefetch=2,
        grid=(n_tiles,),
        in_specs=[
            pl.BlockSpec((ROW_TILE, D), lambda i, te, tv: (i, 0)),
            pl.BlockSpec((1, D, Hd), lambda i, te, tv: (te[i], 0, 0)),
            pl.BlockSpec((1, D, Hd), lambda i, te, tv: (te[i], 0, 0)),
            pl.BlockSpec((1, Hd, D), lambda i, te, tv: (te[i], 0, 0)),
        ],
        out_specs=pl.BlockSpec((ROW_TILE, D), lambda i, te, tv: (i, 0)),
    )
    return pl.pallas_call(
        _ffn_kernel,
        grid_spec=grid_spec,
        out_shape=jax.ShapeDtypeStruct((R, D), F32),
        compiler_params=_cparams(("arbitrary",)),
        name="expert_ffn",
    )(tile_e, tile_valid, xs, wg, wu, wd)


def _combine_kernel(pos_ref, x1_ref, rt_ref, g_ref, y_hbm, o_ref, buf_ref, sem):
    tm = x1_ref.shape[0]

    def start(r, _):
        _row_copy(y_hbm, pos_ref[0, 0, 2 * r], buf_ref.at[0], r, sem).start()
        _row_copy(y_hbm, pos_ref[0, 0, 2 * r + 1], buf_ref.at[1], r, sem).start()
        return 0

    def wait(r, _):
        _row_copy(y_hbm, 0, buf_ref.at[0], r, sem).wait()
        _row_copy(y_hbm, 0, buf_ref.at[1], r, sem).wait()
        return 0

    lax.fori_loop(0, tm, start, 0)
    lax.fori_loop(0, tm, wait, 0)
    rt = rt_ref[...]
    x = x1_ref[...] + rt[:, 2:3] * buf_ref[0] + rt[:, 3:4] * buf_ref[1]
    ms = jnp.mean(x * x, axis=-1, keepdims=True)
    o_ref[...] = x * lax.rsqrt(ms + RMS_EPS) * g_ref[...]


def _combine(x1, routed, pos, g, y, tm=256):
    T, D = x1.shape
    tm = min(tm, T)
    return pl.pallas_call(
        _combine_kernel,
        grid=(T // tm,),
        in_specs=[
            pl.BlockSpec((1, 1, 2 * tm), lambda i: (i, 0, 0), memory_space=pltpu.SMEM),
            pl.BlockSpec((tm, D), lambda i: (i, 0)),
            pl.BlockSpec((tm, LANES), lambda i: (i, 0)),
            pl.BlockSpec((1, D), lambda i: (0, 0)),
            pl.BlockSpec(memory_space=pl.ANY),
        ],
        out_specs=pl.BlockSpec((tm, D), lambda i: (i, 0)),
        out_shape=jax.ShapeDtypeStruct((T, D), F32),
        scratch_shapes=[pltpu.VMEM((2, tm, D), F32), pltpu.SemaphoreType.DMA(())],
        compiler_params=_cparams(("arbitrary",)),
        name="combine",
    )(pos.reshape(T // tm, 1, 2 * tm), x1, routed, g, y)


def _sorted_layout(eid, n_tiles):
    E, TR = N_EXPERTS, ROW_TILE
    eflat = eid.reshape(-1)
    A = eflat.shape[0]
    onehot = (eflat[:, None] == jnp.arange(E, dtype=jnp.int32)[None, :]).astype(jnp.int32)
    csum = jnp.cumsum(onehot, axis=0)
    counts = csum[-1]
    rank = jnp.sum(csum * onehot, axis=1) - 1
    ptiles = (counts + TR - 1) // TR
    tile_end = jnp.cumsum(ptiles)
    pad_off = (tile_end - ptiles) * TR
    unp_off = jnp.cumsum(counts) - counts
    pos = pad_off[eflat] + rank
    order = jnp.argsort(eflat, stable=True).astype(jnp.int32)
    tiles = jnp.arange(n_tiles, dtype=jnp.int32)
    tile_e_raw = jnp.searchsorted(tile_end, tiles, side="right").astype(jnp.int32)
    tile_valid = (tile_e_raw < E).astype(jnp.int32)
    tile_e = jnp.minimum(tile_e_raw, E - 1)
    rows = jnp.arange(n_tiles * TR, dtype=jnp.int32)
    row_e = tile_e[rows // TR]
    k = rows - pad_off[row_e]
    row_ok = jnp.logical_and(tile_valid[rows // TR] != 0, k < counts[row_e])
    slot = jnp.clip(unp_off[row_e] + k, 0, A - 1)
    src_tok = jnp.where(row_ok, order[slot] // 2, 0).astype(jnp.int32)
    return src_tok, pos.astype(jnp.int32), tile_e, tile_valid


def kernel(x, norm_attn_g, w_in, conv_w, conv_b, lru_wa, lru_ba, lru_wx, lru_bx, lru_lambda, proj_rec, proj_attn, w_out, norm_ffn_g, router_group_w, router_group_b, router_expert_w, router_expert_b, expert_w_gate, expert_w_up, expert_w_down, norm_final_g):
    B, S, D = x.shape
    T = B * S
    depth = w_in.shape[0]
    W = lru_lambda.shape[1]
    AW = ATTN_HEADS * HEAD_DIM
    assert W % HEAD_DIM == 0 and W == AW and S % MOBA_BLOCK == 0
    q_col = 2 * W // HEAD_DIM
    k_col = q_col + ATTN_HEADS
    v_col = k_col + ATTN_HEADS
    gate_col = (2 * W + 3 * AW) // (D // 2)
    assert gate_col * (D // 2) == 2 * W + 3 * AW
    slopes = 2.0 ** (-8.0 * jnp.arange(1, ATTN_HEADS + 1, dtype=F32) / ATTN_HEADS)
    n_tiles = (2 * T + N_EXPERTS * (ROW_TILE - 1)) // ROW_TILE + 1
    row = lambda v: v.reshape(1, -1)

    assert depth == 1, "the final norm is fused into the single layer's combine"
    l = 0
    x2 = x.reshape(T, D)
    u = _inproj(x2, row(norm_attn_g[l]), w_in[l].astype(BF16))
    u3 = u.reshape(B, S, -1)
    y_rec = _rglru(u3, conv_w[l], row(conv_b[l]), lru_wa[l].astype(BF16), row(lru_ba[l]),
                   lru_wx[l].astype(BF16), row(lru_bx[l]), row(lru_lambda[l]), W)
    y_attn = _moba(u3, slopes, q_col, k_col, v_col)
    wr = jnp.zeros((D, LANES), F32)
    wr = wr.at[:, :N_GROUPS].set(router_group_w[l]).at[:, N_GROUPS:N_GROUPS + N_EXPERTS].set(router_expert_w[l])
    br = jnp.zeros((1, LANES), F32)
    br = br.at[0, :N_GROUPS].set(router_group_b[l]).at[0, N_GROUPS:N_GROUPS + N_EXPERTS].set(router_expert_b[l])
    x1, h2, logits = _merge(y_rec.reshape(T, W), y_attn.reshape(T, AW), u, gate_col, x2,
                            proj_rec[l].astype(BF16), proj_attn[l].astype(BF16),
                            w_out[l].astype(BF16), row(norm_ffn_g[l]), wr, br)
    routed = _route(logits)
    eid = routed[:, :2].astype(jnp.int32)
    src_tok, pos, tile_e, tile_valid = _sorted_layout(eid, n_tiles)
    xs = _dispatch(h2, src_tok, n_tiles)
    y = _expert_ffn(xs, tile_e, tile_valid, expert_w_gate[l].astype(BF16),
                    expert_w_up[l].astype(BF16), expert_w_down[l].astype(BF16))
    out = _combine(x1, routed, pos, row(norm_final_g), y)
    return out.reshape(B, S, D)
```

```python
import functools

import jax
import jax.numpy as jnp
from jax import lax
from jax.experimental import pallas as pl
from jax.experimental.pallas import tpu as pltpu

F32 = jnp.float32
BF16 = jnp.bfloat16

RMS_EPS = 1e-6
LRU_BLOCKS = 8
CONV_WIDTH = 4
LRU_C = 8.0
ATTN_HEADS = 8
HEAD_DIM = 128
MOBA_BLOCK = 256
MOBA_TOPK = 3
N_GROUPS = 4
EXPERTS_PER_GROUP = 8
N_EXPERTS = N_GROUPS * EXPERTS_PER_GROUP

LANES = 128
SUBLANES = 8
NEG_BIG = -1e30
ROW_TILE = 256
VMEM_LIMIT = 56 * 1024 * 1024


def _cparams(sem):
    return pltpu.CompilerParams(dimension_semantics=sem, vmem_limit_bytes=VMEM_LIMIT)


def _const_spec(shape):
    nd = len(shape)
    return pl.BlockSpec(shape, lambda *_: (0,) * nd, pipeline_mode=pl.Buffered(1))


def _inproj_kernel(x_ref, g_ref, w_ref, o_ref, h_ref, *, chunk):
    tm = x_ref.shape[0]

    @pl.when(pl.program_id(1) == 0)
    def _():
        def body(c, _):
            r0 = pl.multiple_of(c * chunk, chunk)
            x = x_ref[pl.ds(r0, chunk), :]
            ms = jnp.mean(x * x, axis=-1, keepdims=True)
            h_ref[pl.ds(r0, chunk), :] = (x * lax.rsqrt(ms + RMS_EPS) * g_ref[...]).astype(BF16)
            return 0
        lax.fori_loop(0, tm // chunk, body, 0)

    o_ref[...] = jnp.dot(h_ref[...], w_ref[...], preferred_element_type=F32).astype(o_ref.dtype)


def _inproj(x2, g, w_bf16, tm=1024, tn=1024):
    T, D = x2.shape
    N = w_bf16.shape[1]
    tm = min(tm, T)
    return pl.pallas_call(
        functools.partial(_inproj_kernel, chunk=min(256, tm)),
        grid=(T // tm, N // tn),
        in_specs=[
            pl.BlockSpec((tm, D), lambda i, j: (i, 0)),
            pl.BlockSpec((1, D), lambda i, j: (0, 0)),
            pl.BlockSpec((D, tn), lambda i, j: (0, j)),
        ],
        out_specs=pl.BlockSpec((tm, tn), lambda i, j: (i, j)),
        out_shape=jax.ShapeDtypeStruct((T, N), BF16),
        scratch_shapes=[pltpu.VMEM((tm, D), BF16)],
        compiler_params=_cparams(("parallel", "arbitrary")),
        name="inproj",
    )(x2, g, w_bf16)


def _expm1(y, u):
    near = (u - 1.0) * y / jnp.log(u)
    return jnp.where(u == 1.0, y, jnp.where(y > -0.5, near, u - 1.0))


def _rglru_kernel(xr_ref, gr_ref, cw_ref, cb_ref, wa_ref, ba_ref, wx_ref, bx_ref, lam_ref,
                  o_ref, xbuf_ref, hc_ref):
    ts = xr_ref.shape[1]
    W = xr_ref.shape[2]
    bd = W // LRU_BLOCKS
    pad = SUBLANES

    @pl.when(pl.program_id(1) == 0)
    def _():
        xbuf_ref[0:pad, :] = jnp.zeros((pad, W), F32)
        hc_ref[...] = jnp.zeros_like(hc_ref)

    xbuf_ref[pad:pad + ts, :] = xr_ref[0].astype(F32)
    xc = cb_ref[...] + cw_ref[0:1, :] * xbuf_ref[pl.ds(pad - 3, ts), :]
    for k in range(1, CONV_WIDTH):
        xc = xc + cw_ref[k:k + 1, :] * xbuf_ref[pl.ds(pad - 3 + k, ts), :]
    xbuf_ref[0:pad, :] = xbuf_ref[ts:ts + pad, :]

    xcb = xc.astype(BF16)
    ra, ri = [], []
    for n in range(LRU_BLOCKS):
        blk = xcb[:, n * bd:(n + 1) * bd]
        ra.append(jnp.dot(blk, wa_ref[n], preferred_element_type=F32))
        ri.append(jnp.dot(blk, wx_ref[n], preferred_element_type=F32))
    r = jax.nn.sigmoid(jnp.concatenate(ra, axis=-1) + ba_ref[...])
    i = jax.nn.sigmoid(jnp.concatenate(ri, axis=-1) + bx_ref[...])
    lam = lam_ref[...]
    log_sig = jnp.minimum(lam, 0.0) - jnp.log1p(jnp.exp(-jnp.abs(lam)))
    log_a = LRU_C * r * log_sig
    a = jnp.exp(log_a)
    b = jnp.sqrt(-_expm1(2.0 * log_a, a * a)) * (i * xc)

    row = lax.broadcasted_iota(jnp.int32, (ts, W), 0)
    d = 1
    while d < ts:
        keep = row >= d
        a_sh = jnp.where(keep, pltpu.roll(a, d, 0), 1.0)
        b_sh = jnp.where(keep, pltpu.roll(b, d, 0), 0.0)
        b = a * b_sh + b
        a = a * a_sh
        d *= 2
    h = b + a * hc_ref[...]
    hc_ref[...] = h[ts - 1:ts, :]

    o_ref[0] = (h * jax.nn.gelu(gr_ref[0].astype(F32), approximate=True)).astype(o_ref.dtype)


def _rglru(u3, conv_w, conv_b, wa, ba, wx, bx, lam, W, ts=256):
    B, S, _ = u3.shape
    ts = min(ts, S)
    vec = lambda: pl.BlockSpec((1, W), lambda b, s: (0, 0))
    bd = W // LRU_BLOCKS
    return pl.pallas_call(
        _rglru_kernel,
        grid=(B, S // ts),
        in_specs=[
            pl.BlockSpec((1, ts, W), lambda b, s: (b, s, 0)),
            pl.BlockSpec((1, ts, W), lambda b, s: (b, s, 1)),
            pl.BlockSpec((CONV_WIDTH, W), lambda b, s: (0, 0)),
            vec(),
            pl.BlockSpec((LRU_BLOCKS, bd, bd), lambda b, s: (0, 0, 0)),
            vec(),
            pl.BlockSpec((LRU_BLOCKS, bd, bd), lambda b, s: (0, 0, 0)),
            vec(),
            vec(),
        ],
        out_specs=pl.BlockSpec((1, ts, W), lambda b, s: (b, s, 0)),
        out_shape=jax.ShapeDtypeStruct((B, S, W), BF16),
        scratch_shapes=[pltpu.VMEM((SUBLANES + ts, W), F32), pltpu.VMEM((1, W), F32)],
        compiler_params=_cparams(("parallel", "arbitrary")),
        name="rglru",
    )(u3, u3, conv_w, conv_b, wa, ba, wx, bx, lam)


LOG2E = 1.4426950408889634
MOBA_GROUP = 4
MOBA_HEADS_PER_STEP = 2


def _split3(c):
    c0 = c.astype(BF16).astype(F32)
    r1 = c - c0
    c1 = r1.astype(BF16).astype(F32)
    c2 = (r1 - c1).astype(BF16).astype(F32)
    return c0, c1, c2


def _moba_kernel(slope_ref, q_ref, k_ref, v_ref, o_ref,
                 kaug_ref, vt_ref, kmean_ref, qaug_ref, causal_ref, m_ref, l_ref, acc_ref,
                 za_ref, zb_ref, *, G, HB):
    BLK, dh, EXT = MOBA_BLOCK, HEAD_DIM, LANES
    nb = k_ref.shape[1] // BLK
    j = pl.program_id(2)
    heads = range(HB)
    cols = lambda hh: slice(hh * dh, (hh + 1) * dh)

    @pl.when(j == 0)
    def _():
        lane = lax.broadcasted_iota(jnp.int32, (BLK, EXT), 1)
        rowi = lax.broadcasted_iota(jnp.int32, (BLK, EXT), 0)
        for hh in heads:
            slope = slope_ref[pl.program_id(1) * HB + hh]
            for n in range(nb):
                rows = slice(n * BLK, (n + 1) * BLK)
                kb = k_ref[0, rows, cols(hh)]
                kmean_ref[hh, n:n + 1, :] = jnp.mean(kb.astype(F32), axis=0, keepdims=True)
                c0, c1, c2 = _split3((LOG2E * slope) * (rowi + n * BLK).astype(F32))
                ext = jnp.where(lane == n, 1.0,
                                jnp.where(lane == nb, c0, jnp.where(lane == nb + 1, c1,
                                                                    jnp.where(lane == nb + 2, c2, 0.0))))
                kaug_ref[hh, rows, 0:dh] = kb
                kaug_ref[hh, rows, dh:dh + EXT] = ext.astype(BF16)
                vt_ref[hh, n // G, :, (n % G) * BLK:(n % G + 1) * BLK] = (
                    v_ref[0, rows, cols(hh)].astype(F32).T.astype(BF16))
        kk = lax.broadcasted_iota(jnp.int32, (BLK, BLK), 0)
        qq = lax.broadcasted_iota(jnp.int32, (BLK, BLK), 1)
        causal_ref[...] = jnp.where(kk <= qq, 0.0, NEG_BIG)

    blk_i = lax.broadcasted_iota(jnp.int32, (nb, BLK), 0)
    past = blk_i < j
    tail_row = lax.broadcasted_iota(jnp.int32, (EXT - nb, BLK), 0)
    ones3 = jnp.where(tail_row < 3, 1.0, 0.0)
    for hh in heads:
        qf = q_ref[0, :, cols(hh)].astype(F32)
        gate = lax.dot_general(kmean_ref[hh], qf, (((1,), (1,)), ((), ())),
                               preferred_element_type=F32,
                               precision=lax.Precision.HIGHEST)
        g = jnp.where(past, gate, -jnp.inf)
        sel = blk_i == j
        for _ in range(min(MOBA_TOPK, nb)):
            cmax = jnp.max(g, axis=0, keepdims=True)
            first = jnp.min(jnp.where(g == cmax, blk_i, nb), axis=0, keepdims=True)
            hit = blk_i == first
            sel = jnp.logical_or(sel, jnp.logical_and(hit, past))
            g = jnp.where(hit, -jnp.inf, g)
        mask = jnp.where(sel, 0.0, NEG_BIG)
        qaug_ref[hh] = jnp.concatenate([(qf * (HEAD_DIM ** -0.5 * LOG2E)).T, mask, ones3],
                                       axis=0).astype(BF16)

    def scores(hh, gi):
        kg = kaug_ref[hh, pl.ds(pl.multiple_of(gi * (G * BLK), G * BLK), G * BLK), :]
        return jnp.dot(kg, qaug_ref[hh], preferred_element_type=F32)

    def update(hh, gi, z):
        m_old = m_ref[hh]
        m_new = jnp.maximum(m_old, jnp.max(z, axis=0, keepdims=True))
        alpha = jnp.exp2(m_old - m_new)
        p = jnp.exp2(z - m_new)
        m_ref[hh] = m_new
        l_ref[hh] = alpha * l_ref[hh] + jnp.sum(p, axis=0, keepdims=True)
        acc_ref[hh] = alpha * acc_ref[hh] + jnp.dot(vt_ref[hh, gi], p.astype(BF16),
                                                    preferred_element_type=F32)

    g_own = j // G
    last = jnp.maximum(g_own - 1, 0)
    in_grp = lax.broadcasted_iota(jnp.int32, (G, BLK, BLK), 0)
    own_mask = jnp.where(in_grp == j % G, causal_ref[...][None], 0.0)
    z_own = [scores(hh, g_own) for hh in heads]
    for hh in heads:
        za_ref[hh] = scores(hh, 0)
    for hh in heads:
        z = (z_own[hh].reshape(G, BLK, BLK) + own_mask).reshape(G * BLK, BLK)
        m0 = jnp.max(z, axis=0, keepdims=True)
        p = jnp.exp2(z - m0)
        m_ref[hh] = m0
        l_ref[hh] = jnp.sum(p, axis=0, keepdims=True)
        acc_ref[hh] = jnp.dot(vt_ref[hh, g_own], p.astype(BF16), preferred_element_type=F32)

    def body(t, _):
        g0 = 2 * t
        for hh in heads:
            zb_ref[hh] = scores(hh, jnp.minimum(g0 + 1, last))
        for hh in heads:
            update(hh, g0, za_ref[hh])

        @pl.when(g0 + 1 < g_own)
        def _():
            for hh in heads:
                za_ref[hh] = scores(hh, jnp.minimum(g0 + 2, last))
            for hh in heads:
                update(hh, g0 + 1, zb_ref[hh])
        return 0

    lax.fori_loop(0, (g_own + 1) // 2, body, 0)
    for hh in heads:
        o_ref[0, :, cols(hh)] = (acc_ref[hh] / l_ref[hh]).T.astype(o_ref.dtype)


def _moba(u3, slopes, q_col, k_col, v_col):
    B, S, _ = u3.shape
    BLK, dh, H, EXT = MOBA_BLOCK, HEAD_DIM, ATTN_HEADS, LANES
    nb = S // BLK
    G = min(MOBA_GROUP, nb)
    HB = MOBA_HEADS_PER_STEP
    assert nb % G == 0 and nb % SUBLANES == 0 and nb + 3 <= EXT
    assert H % HB == 0 and q_col % HB == 0 and k_col % HB == 0 and v_col % HB == 0
    return pl.pallas_call(
        functools.partial(_moba_kernel, G=G, HB=HB),
        grid=(B, H // HB, nb),
        in_specs=[
            pl.BlockSpec(memory_space=pltpu.SMEM),
            pl.BlockSpec((1, BLK, HB * dh), lambda b, h, j: (b, j, q_col // HB + h)),
            pl.BlockSpec((1, S, HB * dh), lambda b, h, j: (b, 0, k_col // HB + h)),
            pl.BlockSpec((1, S, HB * dh), lambda b, h, j: (b, 0, v_col // HB + h)),
        ],
        out_specs=pl.BlockSpec((1, BLK, HB * dh), lambda b, h, j: (b, j, h)),
        out_shape=jax.ShapeDtypeStruct((B, S, H * dh), BF16),
        scratch_shapes=[
            pltpu.VMEM((HB, S, dh + EXT), BF16),
            pltpu.VMEM((HB, nb // G, dh, G * BLK), BF16),
            pltpu.VMEM((HB, nb, dh), F32),
            pltpu.VMEM((HB, dh + EXT, BLK), BF16),
            pltpu.VMEM((BLK, BLK), F32),
            pltpu.VMEM((HB, 1, BLK), F32),
            pltpu.VMEM((HB, 1, BLK), F32),
            pltpu.VMEM((HB, dh, BLK), F32),
            pltpu.VMEM((HB, G * BLK, BLK), F32),
            pltpu.VMEM((HB, G * BLK, BLK), F32),
        ],
        compiler_params=_cparams(("parallel", "parallel", "arbitrary")),
        name="moba",
    )(slopes, u3, u3, u3)


def _merge_kernel(yr_ref, ya_ref, gr0_ref, gr1_ref, ga0_ref, ga1_ref, x_ref,
                  pr_ref, pa_ref, wo_ref, g_ref, wr_ref, br_ref,
                  x1_ref, h2_ref, lg_ref):
    half = gr0_ref.shape[1]
    pr = jnp.dot(yr_ref[...], pr_ref[...], preferred_element_type=F32)
    pa = jnp.dot(ya_ref[...], pa_ref[...], preferred_element_type=F32)
    sig = lambda ref: jax.nn.sigmoid(ref[...].astype(F32))
    m0 = sig(gr0_ref) * pr[:, :half] + sig(ga0_ref) * pa[:, :half]
    m1 = sig(gr1_ref) * pr[:, half:] + sig(ga1_ref) * pa[:, half:]
    merged = jnp.concatenate([m0, m1], axis=-1).astype(BF16)
    x1 = x_ref[...] + jnp.dot(merged, wo_ref[...], preferred_element_type=F32)
    x1_ref[...] = x1
    ms = jnp.mean(x1 * x1, axis=-1, keepdims=True)
    h2 = x1 * lax.rsqrt(ms + RMS_EPS) * g_ref[...]
    h2_ref[...] = h2
    lg_ref[...] = jnp.dot(h2, wr_ref[...], preferred_element_type=F32,
                          precision=lax.Precision.HIGHEST) + br_ref[...]


def _merge(y_rec, y_attn, u, gate_col, x2, pr, pa, wo, g, wr, br, tm=256):
    T, D = x2.shape
    W = y_rec.shape[1]
    half = D // 2
    tm = min(tm, T)
    gspec = lambda c: pl.BlockSpec((tm, half), lambda i: (i, gate_col + c))
    row = lambda n: pl.BlockSpec((tm, n), lambda i: (i, 0))
    return pl.pallas_call(
        _merge_kernel,
        grid=(T // tm,),
        in_specs=[
            row(W), row(W), gspec(0), gspec(1), gspec(2), gspec(3), row(D),
            _const_spec(pr.shape), _const_spec(pa.shape), _const_spec(wo.shape),
            _const_spec(g.shape), _const_spec(wr.shape), _const_spec(br.shape),
        ],
        out_specs=[row(D), row(D), row(LANES)],
        out_shape=[
            jax.ShapeDtypeStruct((T, D), F32),
            jax.ShapeDtypeStruct((T, D), F32),
            jax.ShapeDtypeStruct((T, LANES), F32),
        ],
        compiler_params=_cparams(("parallel",)),
        name="merge",
    )(y_rec, y_attn, u, u, u, u, x2, pr, pa, wo, g, wr, br)


def _route_kernel(lg_ref, o_ref):
    lg = lg_ref[...]
    G, EPG = N_GROUPS, EXPERTS_PER_GROUP
    lane = lax.broadcasted_iota(jnp.int32, lg.shape, 1)

    def first_argmax(vals, mask):
        v = jnp.where(mask, vals, -jnp.inf)
        mx = jnp.max(v, axis=-1, keepdims=True)
        idx = jnp.min(jnp.where(jnp.logical_and(mask, v == mx), lane, LANES), axis=-1, keepdims=True)
        return mx, idx

    gmask = lane < G
    gmax, gi = first_argmax(lg, gmask)
    gexp = jnp.where(gmask, jnp.exp(lg - gmax), 0.0)
    grp_w = 1.0 / jnp.sum(gexp, axis=-1, keepdims=True)
    lo = G + gi * EPG
    emask = jnp.logical_and(lane >= lo, lane < lo + EPG)
    emax, i1 = first_argmax(lg, emask)
    eexp = jnp.where(emask, jnp.exp(lg - emax), 0.0)
    esum = jnp.sum(eexp, axis=-1, keepdims=True)
    p1 = 1.0 / esum
    mask2 = jnp.logical_and(emask, lane != i1)
    l2, i2 = first_argmax(lg, mask2)
    p2 = jnp.exp(l2 - emax) / esum
    den = p1 + p2
    w1 = grp_w * (p1 / den)
    w2 = grp_w * (p2 / den)
    e1 = (i1 - G).astype(F32)
    e2 = (i2 - G).astype(F32)
    out = jnp.where(lane == 0, e1, jnp.where(lane == 1, e2, jnp.where(lane == 2, w1, jnp.where(lane == 3, w2, 0.0))))
    o_ref[...] = out


def _route(logits, tm=512):
    T = logits.shape[0]
    tm = min(tm, T)
    spec = pl.BlockSpec((tm, LANES), lambda i: (i, 0))
    return pl.pallas_call(
        _route_kernel,
        grid=(T // tm,),
        in_specs=[spec],
        out_specs=spec,
        out_shape=jax.ShapeDtypeStruct((T, LANES), F32),
        compiler_params=_cparams(("parallel",)),
        name="route",
    )(logits)


def _row_copy(src_hbm, src_row, dst_ref, dst_row, sem):
    return pltpu.make_async_copy(src_hbm.at[pl.ds(src_row, 1), :], dst_ref.at[pl.ds(dst_row, 1), :], sem)


def _dispatch_kernel(idx_ref, h_hbm, o_ref, sem):
    n = o_ref.shape[0]

    def start(r, _):
        _row_copy(h_hbm, idx_ref[0, 0, r], o_ref, r, sem).start()
        return 0

    def wait(r, _):
        _row_copy(h_hbm, 0, o_ref, r, sem).wait()
        return 0

    lax.fori_loop(0, n, start, 0)
    lax.fori_loop(0, n, wait, 0)


def _dispatch(h2, src_tok, n_tiles):
    T, D = h2.shape
    return pl.pallas_call(
        _dispatch_kernel,
        grid=(n_tiles,),
        in_specs=[
            pl.BlockSpec((1, 1, ROW_TILE), lambda i: (i, 0, 0), memory_space=pltpu.SMEM),
            pl.BlockSpec(memory_space=pl.ANY),
        ],
        out_specs=pl.BlockSpec((ROW_TILE, D), lambda i: (i, 0)),
        out_shape=jax.ShapeDtypeStruct((n_tiles * ROW_TILE, D), F32),
        scratch_shapes=[pltpu.SemaphoreType.DMA(())],
        compiler_params=_cparams(("arbitrary",)),
        name="dispatch",
    )(src_tok.reshape(n_tiles, 1, ROW_TILE), h2)


def _ffn_kernel(te_ref, tv_ref, x_ref, wg_ref, wu_ref, wd_ref, o_ref):
    i = pl.program_id(0)

    @pl.when(tv_ref[i] != 0)
    def _():
        x = x_ref[...].astype(BF16)
        g = jnp.dot(x, wg_ref[0], preferred_element_type=F32)
        u = jnp.dot(x, wu_ref[0], preferred_element_type=F32)
        hmid = (jax.nn.silu(g) * u).astype(BF16)
        o_ref[...] = jnp.dot(hmid, wd_ref[0], preferred_element_type=F32)

    @pl.when(tv_ref[i] == 0)
    def _():
        o_ref[...] = jnp.zeros_like(o_ref)


def _expert_ffn(xs, tile_e, tile_valid, wg, wu, wd):
    R, D = xs.shape
    Hd = wg.shape[2]
    n_tiles = R // ROW_TILE
    grid_spec = pltpu.PrefetchScalarGridSpec(
        num_scalar_prefetch=2,
        grid=(n_tiles,),
        in_specs=[
            pl.BlockSpec((ROW_TILE, D), lambda i, te, tv: (i, 0)),
            pl.BlockSpec((1, D, Hd), lambda i, te, tv: (te[i], 0, 0)),
            pl.BlockSpec((1, D, Hd), lambda i, te, tv: (te[i], 0, 0)),
            pl.BlockSpec((1, Hd, D), lambda i, te, tv: (te[i], 0, 0)),
        ],
        out_specs=pl.BlockSpec((ROW_TILE, D), lambda i, te, tv: (i, 0)),
    )
    return pl.pallas_call(
        _ffn_kernel,
        grid_spec=grid_spec,
        out_shape=jax.ShapeDtypeStruct((R, D), F32),
        compiler_params=_cparams(("arbitrary",)),
        name="expert_ffn",
    )(tile_e, tile_valid, xs, wg, wu, wd)


def _combine_kernel(pos_ref, x1_ref, rt_ref, g_ref, y_hbm, o_ref, buf_ref, sem):
    tm = x1_ref.shape[0]

    def start(r, _):
        _row_copy(y_hbm, pos_ref[0, 0, 2 * r], buf_ref.at[0], r, sem).start()
        _row_copy(y_hbm, pos_ref[0, 0, 2 * r + 1], buf_ref.at[1], r, sem).start()
        return 0

    def wait(r, _):
        _row_copy(y_hbm, 0, buf_ref.at[0], r, sem).wait()
        _row_copy(y_hbm, 0, buf_ref.at[1], r, sem).wait()
        return 0

    lax.fori_loop(0, tm, start, 0)
    lax.fori_loop(0, tm, wait, 0)
    rt = rt_ref[...]
    x = x1_ref[...] + rt[:, 2:3] * buf_ref[0] + rt[:, 3:4] * buf_ref[1]
    ms = jnp.mean(x * x, axis=-1, keepdims=True)
    o_ref[...] = x * lax.rsqrt(ms + RMS_EPS) * g_ref[...]


def _combine(x1, routed, pos, g, y, tm=256):
    T, D = x1.shape
    tm = min(tm, T)
    return pl.pallas_call(
        _combine_kernel,
        grid=(T // tm,),
        in_specs=[
            pl.BlockSpec((1, 1, 2 * tm), lambda i: (i, 0, 0), memory_space=pltpu.SMEM),
            pl.BlockSpec((tm, D), lambda i: (i, 0)),
            pl.BlockSpec((tm, LANES), lambda i: (i, 0)),
            pl.BlockSpec((1, D), lambda i: (0, 0)),
            pl.BlockSpec(memory_space=pl.ANY),
        ],
        out_specs=pl.BlockSpec((tm, D), lambda i: (i, 0)),
        out_shape=jax.ShapeDtypeStruct((T, D), F32),
        scratch_shapes=[pltpu.VMEM((2, tm, D), F32), pltpu.SemaphoreType.DMA(())],
        compiler_params=_cparams(("arbitrary",)),
        name="combine",
    )(pos.reshape(T // tm, 1, 2 * tm), x1, routed, g, y)


def _sorted_layout(eid, n_tiles):
    E, TR = N_EXPERTS, ROW_TILE
    eflat = eid.reshape(-1)
    A = eflat.shape[0]
    onehot = (eflat[:, None] == jnp.arange(E, dtype=jnp.int32)[None, :]).astype(jnp.int32)
    csum = jnp.cumsum(onehot, axis=0)
    counts = csum[-1]
    ptiles = (counts + TR - 1) // TR
    tile_end = jnp.cumsum(ptiles)
    pad_off = (tile_end - ptiles) * TR
    pos = jnp.sum(onehot * (csum - 1 + pad_off[None, :]), axis=1)
    assert n_tiles * TR == A + E * TR
    k = jnp.arange(TR, dtype=jnp.int32)[None, :]
    n_pad = (ptiles * TR - counts)[:, None]
    pad_key = jnp.where(k < n_pad, (pad_off + counts)[:, None] + k, n_tiles * TR)
    keys = jnp.concatenate([pos, pad_key.reshape(-1)])
    toks = jnp.concatenate([jnp.arange(A, dtype=jnp.int32) // 2, jnp.zeros((E * TR,), jnp.int32)])
    _, src_tok = lax.sort_key_val(keys, toks)
    tiles = jnp.arange(n_tiles, dtype=jnp.int32)
    tile_e_raw = jnp.sum((tile_end[None, :] <= tiles[:, None]).astype(jnp.int32), axis=1)
    tile_valid = (tile_e_raw < E).astype(jnp.int32)
    tile_e = jnp.minimum(tile_e_raw, E - 1)
    return src_tok, pos.astype(jnp.int32), tile_e, tile_valid


def kernel(x, norm_attn_g, w_in, conv_w, conv_b, lru_wa, lru_ba, lru_wx, lru_bx, lru_lambda, proj_rec, proj_attn, w_out, norm_ffn_g, router_group_w, router_group_b, router_expert_w, router_expert_b, expert_w_gate, expert_w_up, expert_w_down, norm_final_g):
    B, S, D = x.shape
    T = B * S
    depth = w_in.shape[0]
    W = lru_lambda.shape[1]
    AW = ATTN_HEADS * HEAD_DIM
    assert W % HEAD_DIM == 0 and W == AW and S % MOBA_BLOCK == 0
    q_col = 2 * W // HEAD_DIM
    k_col = q_col + ATTN_HEADS
    v_col = k_col + ATTN_HEADS
    gate_col = (2 * W + 3 * AW) // (D // 2)
    assert gate_col * (D // 2) == 2 * W + 3 * AW
    slopes = 2.0 ** (-8.0 * jnp.arange(1, ATTN_HEADS + 1, dtype=F32) / ATTN_HEADS)
    assert (2 * T) % ROW_TILE == 0
    n_tiles = 2 * T // ROW_TILE + N_EXPERTS
    row = lambda v: v.reshape(1, -1)

    assert depth == 1, "the final norm is fused into the single layer's combine"
    l = 0
    x2 = x.reshape(T, D)
    u = _inproj(x2, row(norm_attn_g[l]), w_in[l].astype(BF16))
    u3 = u.reshape(B, S, -1)
    y_rec = _rglru(u3, conv_w[l], row(conv_b[l]), lru_wa[l].astype(BF16), row(lru_ba[l]),
                   lru_wx[l].astype(BF16), row(lru_bx[l]), row(lru_lambda[l]), W)
    y_attn = _moba(u3, slopes, q_col, k_col, v_col)
    wr = jnp.zeros((D, LANES), F32)
    wr = wr.at[:, :N_GROUPS].set(router_group_w[l]).at[:, N_GROUPS:N_GROUPS + N_EXPERTS].set(router_expert_w[l])
    br = jnp.zeros((1, LANES), F32)
    br = br.at[0, :N_GROUPS].set(router_group_b[l]).at[0, N_GROUPS:N_GROUPS + N_EXPERTS].set(router_expert_b[l])
    x1, h2, logits = _merge(y_rec.reshape(T, W), y_attn.reshape(T, AW), u, gate_col, x2,
                            proj_rec[l].astype(BF16), proj_attn[l].astype(BF16),
                            w_out[l].astype(BF16), row(norm_ffn_g[l]), wr, br)
    routed = _route(logits)
    eid = routed[:, :2].astype(jnp.int32)
    src_tok, pos, tile_e, tile_valid = _sorted_layout(eid, n_tiles)
    xs = _dispatch(h2, src_tok, n_tiles)
    y = _expert_ffn(xs, tile_e, tile_valid, expert_w_gate[l].astype(BF16),
                    expert_w_up[l].astype(BF16), expert_w_down[l].astype(BF16))
    out = _combine(x1, routed, pos, row(norm_final_g), y)
    return out.reshape(B, S, D)
```

```python
import functools

import jax
import jax.numpy as jnp
from jax import lax
from jax.experimental import pallas as pl
from jax.experimental.pallas import tpu as pltpu

F32 = jnp.float32
BF16 = jnp.bfloat16

RMS_EPS = 1e-6
LRU_BLOCKS = 8
CONV_WIDTH = 4
LRU_C = 8.0
ATTN_HEADS = 8
HEAD_DIM = 128
MOBA_BLOCK = 256
MOBA_TOPK = 3
N_GROUPS = 4
EXPERTS_PER_GROUP = 8
N_EXPERTS = N_GROUPS * EXPERTS_PER_GROUP

LANES = 128
SUBLANES = 8
NEG_BIG = -1e30
ROW_TILE = 256
VMEM_LIMIT = 56 * 1024 * 1024


def _cparams(sem):
    return pltpu.CompilerParams(dimension_semantics=sem, vmem_limit_bytes=VMEM_LIMIT)


def _const_spec(shape):
    nd = len(shape)
    return pl.BlockSpec(shape, lambda *_: (0,) * nd, pipeline_mode=pl.Buffered(1))


def _store_token_rows(ref, val):
    n, d = val.shape
    c_n = d // LANES
    for c in range(c_n):
        ref[pl.ds(c, n, stride=c_n), :] = val[:, c * LANES:(c + 1) * LANES]


def _load_token_rows(ref, n):
    c_n = ref.shape[0] // n
    return jnp.concatenate([ref[pl.ds(c, n, stride=c_n), :] for c in range(c_n)], axis=1)


def _inproj_kernel(x_ref, g_ref, w_ref, o_ref, h_ref, *, chunk):
    tm = x_ref.shape[0]

    @pl.when(pl.program_id(1) == 0)
    def _():
        def body(c, _):
            r0 = pl.multiple_of(c * chunk, chunk)
            x = x_ref[pl.ds(r0, chunk), :]
            ms = jnp.mean(x * x, axis=-1, keepdims=True)
            h_ref[pl.ds(r0, chunk), :] = (x * lax.rsqrt(ms + RMS_EPS) * g_ref[...]).astype(BF16)
            return 0
        lax.fori_loop(0, tm // chunk, body, 0)

    o_ref[...] = jnp.dot(h_ref[...], w_ref[...], preferred_element_type=F32).astype(o_ref.dtype)


def _inproj(x2, g, w_bf16, tm=1024, tn=1024):
    T, D = x2.shape
    N = w_bf16.shape[1]
    tm = min(tm, T)
    return pl.pallas_call(
        functools.partial(_inproj_kernel, chunk=min(256, tm)),
        grid=(T // tm, N // tn),
        in_specs=[
            pl.BlockSpec((tm, D), lambda i, j: (i, 0)),
            pl.BlockSpec((1, D), lambda i, j: (0, 0)),
            pl.BlockSpec((D, tn), lambda i, j: (0, j)),
        ],
        out_specs=pl.BlockSpec((tm, tn), lambda i, j: (i, j)),
        out_shape=jax.ShapeDtypeStruct((T, N), BF16),
        scratch_shapes=[pltpu.VMEM((tm, D), BF16)],
        compiler_params=_cparams(("parallel", "arbitrary")),
        name="inproj",
    )(x2, g, w_bf16)


def _expm1(y, u):
    near = (u - 1.0) * y / jnp.log(u)
    return jnp.where(u == 1.0, y, jnp.where(y > -0.5, near, u - 1.0))


def _rglru_kernel(xr_ref, gr_ref, cw_ref, cb_ref, wa_ref, ba_ref, wx_ref, bx_ref, lam_ref,
                  o_ref, xbuf_ref, hc_ref):
    ts = xr_ref.shape[1]
    W = xr_ref.shape[2]
    bd = W // LRU_BLOCKS
    pad = SUBLANES

    @pl.when(pl.program_id(1) == 0)
    def _():
        xbuf_ref[0:pad, :] = jnp.zeros((pad, W), F32)
        hc_ref[...] = jnp.zeros_like(hc_ref)

    xbuf_ref[pad:pad + ts, :] = xr_ref[0].astype(F32)
    xc = cb_ref[...] + cw_ref[0:1, :] * xbuf_ref[pl.ds(pad - 3, ts), :]
    for k in range(1, CONV_WIDTH):
        xc = xc + cw_ref[k:k + 1, :] * xbuf_ref[pl.ds(pad - 3 + k, ts), :]
    xbuf_ref[0:pad, :] = xbuf_ref[ts:ts + pad, :]

    xcb = xc.astype(BF16)
    ra, ri = [], []
    for n in range(LRU_BLOCKS):
        blk = xcb[:, n * bd:(n + 1) * bd]
        ra.append(jnp.dot(blk, wa_ref[n], preferred_element_type=F32))
        ri.append(jnp.dot(blk, wx_ref[n], preferred_element_type=F32))
    r = jax.nn.sigmoid(jnp.concatenate(ra, axis=-1) + ba_ref[...])
    i = jax.nn.sigmoid(jnp.concatenate(ri, axis=-1) + bx_ref[...])
    lam = lam_ref[...]
    log_sig = jnp.minimum(lam, 0.0) - jnp.log1p(jnp.exp(-jnp.abs(lam)))
    log_a = LRU_C * r * log_sig
    a = jnp.exp(log_a)
    b = jnp.sqrt(-_expm1(2.0 * log_a, a * a)) * (i * xc)

    row = lax.broadcasted_iota(jnp.int32, (ts, W), 0)
    d = 1
    while d < ts:
        keep = row >= d
        a_sh = jnp.where(keep, pltpu.roll(a, d, 0), 1.0)
        b_sh = jnp.where(keep, pltpu.roll(b, d, 0), 0.0)
        b = a * b_sh + b
        a = a * a_sh
        d *= 2
    h = b + a * hc_ref[...]
    hc_ref[...] = h[ts - 1:ts, :]

    o_ref[0] = (h * jax.nn.gelu(gr_ref[0].astype(F32), approximate=True)).astype(o_ref.dtype)


def _rglru(u3, conv_w, conv_b, wa, ba, wx, bx, lam, W, ts=256):
    B, S, _ = u3.shape
    ts = min(ts, S)
    vec = lambda: pl.BlockSpec((1, W), lambda b, s: (0, 0))
    bd = W // LRU_BLOCKS
    return pl.pallas_call(
        _rglru_kernel,
        grid=(B, S // ts),
        in_specs=[
            pl.BlockSpec((1, ts, W), lambda b, s: (b, s, 0)),
            pl.BlockSpec((1, ts, W), lambda b, s: (b, s, 1)),
            pl.BlockSpec((CONV_WIDTH, W), lambda b, s: (0, 0)),
            vec(),
            pl.BlockSpec((LRU_BLOCKS, bd, bd), lambda b, s: (0, 0, 0)),
            vec(),
            pl.BlockSpec((LRU_BLOCKS, bd, bd), lambda b, s: (0, 0, 0)),
            vec(),
            vec(),
        ],
        out_specs=pl.BlockSpec((1, ts, W), lambda b, s: (b, s, 0)),
        out_shape=jax.ShapeDtypeStruct((B, S, W), BF16),
        scratch_shapes=[pltpu.VMEM((SUBLANES + ts, W), F32), pltpu.VMEM((1, W), F32)],
        compiler_params=_cparams(("parallel", "arbitrary")),
        name="rglru",
    )(u3, u3, conv_w, conv_b, wa, ba, wx, bx, lam)


LOG2E = 1.4426950408889634
MOBA_GROUP = 4
MOBA_HEADS_PER_STEP = 2


def _split3(c):
    c0 = c.astype(BF16).astype(F32)
    r1 = c - c0
    c1 = r1.astype(BF16).astype(F32)
    c2 = (r1 - c1).astype(BF16).astype(F32)
    return c0, c1, c2


def _moba_kernel(slope_ref, q_ref, k_ref, v_ref, o_ref,
                 kaug_ref, vt_ref, kmean_ref, qaug_ref, causal_ref, m_ref, l_ref, acc_ref,
                 za_ref, zb_ref, *, G, HB):
    BLK, dh, EXT = MOBA_BLOCK, HEAD_DIM, LANES
    nb = k_ref.shape[1] // BLK
    j = pl.program_id(2)
    heads = range(HB)
    cols = lambda hh: slice(hh * dh, (hh + 1) * dh)

    @pl.when(j == 0)
    def _():
        lane = lax.broadcasted_iota(jnp.int32, (BLK, EXT), 1)
        rowi = lax.broadcasted_iota(jnp.int32, (BLK, EXT), 0)
        for hh in heads:
            slope = slope_ref[pl.program_id(1) * HB + hh]
            for n in range(nb):
                rows = slice(n * BLK, (n + 1) * BLK)
                kb = k_ref[0, rows, cols(hh)]
                kmean_ref[hh, n:n + 1, :] = jnp.mean(kb.astype(F32), axis=0, keepdims=True)
                c0, c1, c2 = _split3((LOG2E * slope) * (rowi + n * BLK).astype(F32))
                ext = jnp.where(lane == n, 1.0,
                                jnp.where(lane == nb, c0, jnp.where(lane == nb + 1, c1,
                                                                    jnp.where(lane == nb + 2, c2, 0.0))))
                kaug_ref[hh, rows, 0:dh] = kb
                kaug_ref[hh, rows, dh:dh + EXT] = ext.astype(BF16)
                vt_ref[hh, n // G, :, (n % G) * BLK:(n % G + 1) * BLK] = (
                    v_ref[0, rows, cols(hh)].astype(F32).T.astype(BF16))
        kk = lax.broadcasted_iota(jnp.int32, (BLK, BLK), 0)
        qq = lax.broadcasted_iota(jnp.int32, (BLK, BLK), 1)
        causal_ref[...] = jnp.where(kk <= qq, 0.0, NEG_BIG)

    blk_i = lax.broadcasted_iota(jnp.int32, (nb, BLK), 0)
    past = blk_i < j
    tail_row = lax.broadcasted_iota(jnp.int32, (EXT - nb, BLK), 0)
    ones3 = jnp.where(tail_row < 3, 1.0, 0.0)
    for hh in heads:
        qf = q_ref[0, :, cols(hh)].astype(F32)
        gate = lax.dot_general(kmean_ref[hh], qf, (((1,), (1,)), ((), ())),
                               preferred_element_type=F32,
                               precision=lax.Precision.HIGHEST)
        g = jnp.where(past, gate, -jnp.inf)
        sel = blk_i == j
        for _ in range(min(MOBA_TOPK, nb)):
            cmax = jnp.max(g, axis=0, keepdims=True)
            first = jnp.min(jnp.where(g == cmax, blk_i, nb), axis=0, keepdims=True)
            hit = blk_i == first
            sel = jnp.logical_or(sel, jnp.logical_and(hit, past))
            g = jnp.where(hit, -jnp.inf, g)
        mask = jnp.where(sel, 0.0, NEG_BIG)
        qaug_ref[hh] = jnp.concatenate([(qf * (HEAD_DIM ** -0.5 * LOG2E)).T, mask, ones3],
                                       axis=0).astype(BF16)

    def scores(hh, gi):
        kg = kaug_ref[hh, pl.ds(pl.multiple_of(gi * (G * BLK), G * BLK), G * BLK), :]
        return jnp.dot(kg, qaug_ref[hh], preferred_element_type=F32)

    def update(hh, gi, z):
        m_old = m_ref[hh]
        m_new = jnp.maximum(m_old, jnp.max(z, axis=0, keepdims=True))
        alpha = jnp.exp2(m_old - m_new)
        p = jnp.exp2(z - m_new)
        m_ref[hh] = m_new
        l_ref[hh] = alpha * l_ref[hh] + jnp.sum(p, axis=0, keepdims=True)
        acc_ref[hh] = alpha * acc_ref[hh] + jnp.dot(vt_ref[hh, gi], p.astype(BF16),
                                                    preferred_element_type=F32)

    g_own = j // G
    last = jnp.maximum(g_own - 1, 0)
    in_grp = lax.broadcasted_iota(jnp.int32, (G, BLK, BLK), 0)
    own_mask = jnp.where(in_grp == j % G, causal_ref[...][None], 0.0)
    z_own = [scores(hh, g_own) for hh in heads]
    for hh in heads:
        za_ref[hh] = scores(hh, 0)
    for hh in heads:
        z = (z_own[hh].reshape(G, BLK, BLK) + own_mask).reshape(G * BLK, BLK)
        m0 = jnp.max(z, axis=0, keepdims=True)
        p = jnp.exp2(z - m0)
        m_ref[hh] = m0
        l_ref[hh] = jnp.sum(p, axis=0, keepdims=True)
        acc_ref[hh] = jnp.dot(vt_ref[hh, g_own], p.astype(BF16), preferred_element_type=F32)

    def body(t, _):
        g0 = 2 * t
        for hh in heads:
            zb_ref[hh] = scores(hh, jnp.minimum(g0 + 1, last))
        for hh in heads:
            update(hh, g0, za_ref[hh])

        @pl.when(g0 + 1 < g_own)
        def _():
            for hh in heads:
                za_ref[hh] = scores(hh, jnp.minimum(g0 + 2, last))
            for hh in heads:
                update(hh, g0 + 1, zb_ref[hh])
        return 0

    lax.fori_loop(0, (g_own + 1) // 2, body, 0)
    for hh in heads:
        o_ref[0, :, cols(hh)] = (acc_ref[hh] / l_ref[hh]).T.astype(o_ref.dtype)


def _moba(u3, slopes, q_col, k_col, v_col):
    B, S, _ = u3.shape
    BLK, dh, H, EXT = MOBA_BLOCK, HEAD_DIM, ATTN_HEADS, LANES
    nb = S // BLK
    G = min(MOBA_GROUP, nb)
    HB = MOBA_HEADS_PER_STEP
    assert nb % G == 0 and nb % SUBLANES == 0 and nb + 3 <= EXT
    assert H % HB == 0 and q_col % HB == 0 and k_col % HB == 0 and v_col % HB == 0
    return pl.pallas_call(
        functools.partial(_moba_kernel, G=G, HB=HB),
        grid=(B, H // HB, nb),
        in_specs=[
            pl.BlockSpec(memory_space=pltpu.SMEM),
            pl.BlockSpec((1, BLK, HB * dh), lambda b, h, j: (b, j, q_col // HB + h)),
            pl.BlockSpec((1, S, HB * dh), lambda b, h, j: (b, 0, k_col // HB + h)),
            pl.BlockSpec((1, S, HB * dh), lambda b, h, j: (b, 0, v_col // HB + h)),
        ],
        out_specs=pl.BlockSpec((1, BLK, HB * dh), lambda b, h, j: (b, j, h)),
        out_shape=jax.ShapeDtypeStruct((B, S, H * dh), BF16),
        scratch_shapes=[
            pltpu.VMEM((HB, S, dh + EXT), BF16),
            pltpu.VMEM((HB, nb // G, dh, G * BLK), BF16),
            pltpu.VMEM((HB, nb, dh), F32),
            pltpu.VMEM((HB, dh + EXT, BLK), BF16),
            pltpu.VMEM((BLK, BLK), F32),
            pltpu.VMEM((HB, 1, BLK), F32),
            pltpu.VMEM((HB, 1, BLK), F32),
            pltpu.VMEM((HB, dh, BLK), F32),
            pltpu.VMEM((HB, G * BLK, BLK), F32),
            pltpu.VMEM((HB, G * BLK, BLK), F32),
        ],
        compiler_params=_cparams(("parallel", "parallel", "arbitrary")),
        name="moba",
    )(slopes, u3, u3, u3)


def _merge_kernel(yr_ref, ya_ref, gr0_ref, gr1_ref, ga0_ref, ga1_ref, x_ref,
                  pr_ref, pa_ref, wo_ref, g_ref, wr_ref, br_ref,
                  x1_ref, h2_ref, lg_ref):
    half = gr0_ref.shape[1]
    pr = jnp.dot(yr_ref[...], pr_ref[...], preferred_element_type=F32)
    pa = jnp.dot(ya_ref[...], pa_ref[...], preferred_element_type=F32)
    sig = lambda ref: jax.nn.sigmoid(ref[...].astype(F32))
    m0 = sig(gr0_ref) * pr[:, :half] + sig(ga0_ref) * pa[:, :half]
    m1 = sig(gr1_ref) * pr[:, half:] + sig(ga1_ref) * pa[:, half:]
    merged = jnp.concatenate([m0, m1], axis=-1).astype(BF16)
    x1 = x_ref[...] + jnp.dot(merged, wo_ref[...], preferred_element_type=F32)
    x1_ref[...] = x1
    ms = jnp.mean(x1 * x1, axis=-1, keepdims=True)
    h2 = x1 * lax.rsqrt(ms + RMS_EPS) * g_ref[...]
    _store_token_rows(h2_ref, h2)
    hi = h2.astype(BF16)
    lo = (h2 - hi.astype(F32)).astype(BF16)
    dot = functools.partial(jnp.dot, preferred_element_type=F32)
    lg_ref[...] = dot(hi, wr_ref[0]) + dot(lo, wr_ref[0]) + dot(hi, wr_ref[1]) + br_ref[...]


def _hi_lo(w):
    hi = w.astype(BF16)
    return jnp.stack([hi, (w - hi.astype(F32)).astype(BF16)])


def _merge(y_rec, y_attn, u, gate_col, x2, pr, pa, wo, g, wr, br, tm=256):
    T, D = x2.shape
    W = y_rec.shape[1]
    half = D // 2
    tm = min(tm, T)
    gspec = lambda c: pl.BlockSpec((tm, half), lambda i: (i, gate_col + c))
    row = lambda n: pl.BlockSpec((tm, n), lambda i: (i, 0))
    return pl.pallas_call(
        _merge_kernel,
        grid=(T // tm,),
        in_specs=[
            row(W), row(W), gspec(0), gspec(1), gspec(2), gspec(3), row(D),
            _const_spec(pr.shape), _const_spec(pa.shape), _const_spec(wo.shape),
            _const_spec(g.shape), _const_spec(wr.shape), _const_spec(br.shape),
        ],
        out_specs=[row(D), pl.BlockSpec((tm * (D // LANES), LANES), lambda i: (i, 0)), row(LANES)],
        out_shape=[
            jax.ShapeDtypeStruct((T, D), F32),
            jax.ShapeDtypeStruct((T * (D // LANES), LANES), F32),
            jax.ShapeDtypeStruct((T, LANES), F32),
        ],
        compiler_params=_cparams(("parallel",)),
        name="merge",
    )(y_rec, y_attn, u, u, u, u, x2, pr, pa, wo, g, wr, br)


def _route_kernel(lg_ref, o_ref):
    lg = lg_ref[...]
    G, EPG = N_GROUPS, EXPERTS_PER_GROUP
    lane = lax.broadcasted_iota(jnp.int32, lg.shape, 1)

    def first_argmax(vals, mask):
        v = jnp.where(mask, vals, -jnp.inf)
        mx = jnp.max(v, axis=-1, keepdims=True)
        idx = jnp.min(jnp.where(jnp.logical_and(mask, v == mx), lane, LANES), axis=-1, keepdims=True)
        return mx, idx

    gmask = lane < G
    gmax, gi = first_argmax(lg, gmask)
    gexp = jnp.where(gmask, jnp.exp(lg - gmax), 0.0)
    grp_w = 1.0 / jnp.sum(gexp, axis=-1, keepdims=True)
    lo = G + gi * EPG
    emask = jnp.logical_and(lane >= lo, lane < lo + EPG)
    emax, i1 = first_argmax(lg, emask)
    eexp = jnp.where(emask, jnp.exp(lg - emax), 0.0)
    esum = jnp.sum(eexp, axis=-1, keepdims=True)
    p1 = 1.0 / esum
    mask2 = jnp.logical_and(emask, lane != i1)
    l2, i2 = first_argmax(lg, mask2)
    p2 = jnp.exp(l2 - emax) / esum
    den = p1 + p2
    w1 = grp_w * (p1 / den)
    w2 = grp_w * (p2 / den)
    e1 = (i1 - G).astype(F32)
    e2 = (i2 - G).astype(F32)
    out = jnp.where(lane == 0, e1, jnp.where(lane == 1, e2, jnp.where(lane == 2, w1, jnp.where(lane == 3, w2, 0.0))))
    o_ref[...] = out


def _route(logits, tm=512):
    T = logits.shape[0]
    tm = min(tm, T)
    spec = pl.BlockSpec((tm, LANES), lambda i: (i, 0))
    return pl.pallas_call(
        _route_kernel,
        grid=(T // tm,),
        in_specs=[spec],
        out_specs=spec,
        out_shape=jax.ShapeDtypeStruct((T, LANES), F32),
        compiler_params=_cparams(("parallel",)),
        name="route",
    )(logits)


def _token_copy(src_hbm, src_tok, dst_ref, dst_tok, sem, c_n):
    return pltpu.make_async_copy(src_hbm.at[pl.ds(pl.multiple_of(src_tok * c_n, c_n), c_n), :],
                                 dst_ref.at[pl.ds(dst_tok * c_n, c_n), :], sem)


def _wait_tokens(src_hbm, dst_ref, sem):
    pltpu.make_async_copy(src_hbm.at[pl.ds(0, dst_ref.shape[0]), :], dst_ref, sem).wait()


def _ffn_kernel(te_ref, cur_ref, nxt_ref, h_hbm, wg_ref, wu_ref, wd_ref, o_ref, xbuf_ref, sem):
    i = pl.program_id(0)
    n = pl.num_programs(0)
    tr = cur_ref.shape[2]
    c_n = xbuf_ref.shape[1] // tr
    slot = i % 2

    def gather(idx_ref, s):
        for r in range(tr):
            _token_copy(h_hbm, idx_ref[0, 0, r], xbuf_ref.at[s], r, sem.at[s], c_n).start()

    @pl.when(i == 0)
    def _():
        gather(cur_ref, 0)

    _wait_tokens(h_hbm, xbuf_ref.at[slot], sem.at[slot])
    gather(nxt_ref, 1 - slot)
    x = _load_token_rows(xbuf_ref.at[slot], tr).astype(BF16)
    g = jnp.dot(x, wg_ref[0], preferred_element_type=F32)
    u = jnp.dot(x, wu_ref[0], preferred_element_type=F32)
    hmid = (jax.nn.silu(g) * u).astype(BF16)
    _store_token_rows(o_ref, jnp.dot(hmid, wd_ref[0], preferred_element_type=F32))

    @pl.when(i == n - 1)
    def _():
        _wait_tokens(h_hbm, xbuf_ref.at[1 - slot], sem.at[1 - slot])


def _expert_ffn(h2r, src_tok, tile_e, wg, wu, wd):
    D, Hd = wg.shape[1], wg.shape[2]
    c_n = D // LANES
    n_tiles = tile_e.shape[0]
    idx = src_tok.reshape(n_tiles, 1, ROW_TILE)
    grid_spec = pltpu.PrefetchScalarGridSpec(
        num_scalar_prefetch=1,
        grid=(n_tiles,),
        in_specs=[
            pl.BlockSpec((1, 1, ROW_TILE), lambda i, te: (i, 0, 0), memory_space=pltpu.SMEM),
            pl.BlockSpec((1, 1, ROW_TILE), lambda i, te: (jnp.minimum(i + 1, n_tiles - 1), 0, 0),
                         memory_space=pltpu.SMEM),
            pl.BlockSpec(memory_space=pl.ANY),
            pl.BlockSpec((1, D, Hd), lambda i, te: (te[i], 0, 0)),
            pl.BlockSpec((1, D, Hd), lambda i, te: (te[i], 0, 0)),
            pl.BlockSpec((1, Hd, D), lambda i, te: (te[i], 0, 0)),
        ],
        out_specs=pl.BlockSpec((ROW_TILE * c_n, LANES), lambda i, te: (i, 0)),
        scratch_shapes=[pltpu.VMEM((2, ROW_TILE * c_n, LANES), F32), pltpu.SemaphoreType.DMA((2,))],
    )
    return pl.pallas_call(
        _ffn_kernel,
        grid_spec=grid_spec,
        out_shape=jax.ShapeDtypeStruct((n_tiles * ROW_TILE * c_n, LANES), F32),
        compiler_params=_cparams(("arbitrary",)),
        name="expert_ffn",
    )(tile_e, idx, idx, h2r, wg, wu, wd)


def _combine_kernel(cur_ref, nxt_ref, x1_ref, rt_ref, g_ref, y_hbm, o_ref, buf_ref, sem):
    i = pl.program_id(0)
    n = pl.num_programs(0)
    tm = x1_ref.shape[0]
    c_n = buf_ref.shape[1] // (2 * tm)
    slot = i % 2

    def gather(pos_ref, s):
        for r in range(tm):
            for k in range(2):
                _token_copy(y_hbm, pos_ref[0, 0, 2 * r + k], buf_ref.at[s], k * tm + r, sem.at[s], c_n).start()

    @pl.when(i == 0)
    def _():
        gather(cur_ref, 0)

    _wait_tokens(y_hbm, buf_ref.at[slot], sem.at[slot])
    gather(nxt_ref, 1 - slot)
    y = _load_token_rows(buf_ref.at[slot], 2 * tm)
    rt = rt_ref[...]
    x = x1_ref[...] + rt[:, 2:3] * y[:tm] + rt[:, 3:4] * y[tm:]
    ms = jnp.mean(x * x, axis=-1, keepdims=True)
    o_ref[...] = x * lax.rsqrt(ms + RMS_EPS) * g_ref[...]

    @pl.when(i == n - 1)
    def _():
        _wait_tokens(y_hbm, buf_ref.at[1 - slot], sem.at[1 - slot])


def _combine(x1, routed, pos, g, y, tm=128):
    T, D = x1.shape
    tm = min(tm, T)
    c_n = D // LANES
    n = T // tm
    idx = pos.reshape(n, 1, 2 * tm)
    return pl.pallas_call(
        _combine_kernel,
        grid=(n,),
        in_specs=[
            pl.BlockSpec((1, 1, 2 * tm), lambda i: (i, 0, 0), memory_space=pltpu.SMEM),
            pl.BlockSpec((1, 1, 2 * tm), lambda i: (jnp.minimum(i + 1, n - 1), 0, 0),
                         memory_space=pltpu.SMEM),
            pl.BlockSpec((tm, D), lambda i: (i, 0)),
            pl.BlockSpec((tm, LANES), lambda i: (i, 0)),
            pl.BlockSpec((1, D), lambda i: (0, 0)),
            pl.BlockSpec(memory_space=pl.ANY),
        ],
        out_specs=pl.BlockSpec((tm, D), lambda i: (i, 0)),
        out_shape=jax.ShapeDtypeStruct((T, D), F32),
        scratch_shapes=[pltpu.VMEM((2, 2 * tm * c_n, LANES), F32), pltpu.SemaphoreType.DMA((2,))],
        compiler_params=_cparams(("arbitrary",)),
        name="combine",
    )(idx, idx, x1, routed, g, y)


def _sorted_layout(eid, n_tiles):
    E, TR = N_EXPERTS, ROW_TILE
    eflat = eid.reshape(-1)
    A = eflat.shape[0]
    onehot = (eflat[:, None] == jnp.arange(E, dtype=jnp.int32)[None, :]).astype(jnp.int32)
    csum = jnp.cumsum(onehot, axis=0)
    counts = csum[-1]
    ptiles = (counts + TR - 1) // TR
    tile_end = jnp.cumsum(ptiles)
    pad_off = (tile_end - ptiles) * TR
    pos = jnp.sum(onehot * (csum - 1 + pad_off[None, :]), axis=1)
    assert n_tiles * TR == A + E * TR
    k = jnp.arange(TR, dtype=jnp.int32)[None, :]
    n_pad = (ptiles * TR - counts)[:, None]
    pad_key = jnp.where(k < n_pad, (pad_off + counts)[:, None] + k, n_tiles * TR)
    keys = jnp.concatenate([pos, pad_key.reshape(-1)])
    toks = jnp.concatenate([jnp.arange(A, dtype=jnp.int32) // 2, jnp.zeros((E * TR,), jnp.int32)])
    _, src_tok = lax.sort_key_val(keys, toks)
    tiles = jnp.arange(n_tiles, dtype=jnp.int32)
    tile_e_raw = jnp.sum((tile_end[None, :] <= tiles[:, None]).astype(jnp.int32), axis=1)
    tile_e = jnp.minimum(tile_e_raw, E - 1)
    return src_tok, pos.astype(jnp.int32), tile_e


def kernel(x, norm_attn_g, w_in, conv_w, conv_b, lru_wa, lru_ba, lru_wx, lru_bx, lru_lambda, proj_rec, proj_attn, w_out, norm_ffn_g, router_group_w, router_group_b, router_expert_w, router_expert_b, expert_w_gate, expert_w_up, expert_w_down, norm_final_g):
    B, S, D = x.shape
    T = B * S
    depth = w_in.shape[0]
    W = lru_lambda.shape[1]
    AW = ATTN_HEADS * HEAD_DIM
    assert W % HEAD_DIM == 0 and W == AW and S % MOBA_BLOCK == 0
    q_col = 2 * W // HEAD_DIM
    k_col = q_col + ATTN_HEADS
    v_col = k_col + ATTN_HEADS
    gate_col = (2 * W + 3 * AW) // (D // 2)
    assert gate_col * (D // 2) == 2 * W + 3 * AW
    slopes = 2.0 ** (-8.0 * jnp.arange(1, ATTN_HEADS + 1, dtype=F32) / ATTN_HEADS)
    assert (2 * T) % ROW_TILE == 0
    n_tiles = 2 * T // ROW_TILE + N_EXPERTS
    row = lambda v: v.reshape(1, -1)

    assert depth == 1, "the final norm is fused into the single layer's combine"
    l = 0
    x2 = x.reshape(T, D)
    u = _inproj(x2, row(norm_attn_g[l]), w_in[l].astype(BF16))
    u3 = u.reshape(B, S, -1)
    y_rec = _rglru(u3, conv_w[l], row(conv_b[l]), lru_wa[l].astype(BF16), row(lru_ba[l]),
                   lru_wx[l].astype(BF16), row(lru_bx[l]), row(lru_lambda[l]), W)
    y_attn = _moba(u3, slopes, q_col, k_col, v_col)
    wr = jnp.zeros((D, LANES), F32)
    wr = wr.at[:, :N_GROUPS].set(router_group_w[l]).at[:, N_GROUPS:N_GROUPS + N_EXPERTS].set(router_expert_w[l])
    br = jnp.zeros((1, LANES), F32)
    br = br.at[0, :N_GROUPS].set(router_group_b[l]).at[0, N_GROUPS:N_GROUPS + N_EXPERTS].set(router_expert_b[l])
    x1, h2, logits = _merge(y_rec.reshape(T, W), y_attn.reshape(T, AW), u, gate_col, x2,
                            proj_rec[l].astype(BF16), proj_attn[l].astype(BF16),
                            w_out[l].astype(BF16), row(norm_ffn_g[l]), _hi_lo(wr), br)
    routed = _route(logits)
    eid = routed[:, :2].astype(jnp.int32)
    src_tok, pos, tile_e = _sorted_layout(eid, n_tiles)
    y = _expert_ffn(h2, src_tok, tile_e, expert_w_gate[l].astype(BF16),
                    expert_w_up[l].astype(BF16), expert_w_down[l].astype(BF16))
    out = _combine(x1, routed, pos, row(norm_final_g), y)
    return out.reshape(B, S, D)
```

```python
import functools

import jax
import jax.numpy as jnp
from jax import lax
from jax.experimental import pallas as pl
from jax.experimental.pallas import tpu as pltpu

F32 = jnp.float32
BF16 = jnp.bfloat16

RMS_EPS = 1e-6
LRU_BLOCKS = 8
CONV_WIDTH = 4
LRU_C = 8.0
ATTN_HEADS = 8
HEAD_DIM = 128
MOBA_BLOCK = 256
MOBA_TOPK = 3
N_GROUPS = 4
EXPERTS_PER_GROUP = 8
N_EXPERTS = N_GROUPS * EXPERTS_PER_GROUP

LANES = 128
SUBLANES = 8
NEG_BIG = -1e30
ROW_TILE = 256
VMEM_LIMIT = 56 * 1024 * 1024


def _cparams(sem):
    return pltpu.CompilerParams(dimension_semantics=sem, vmem_limit_bytes=VMEM_LIMIT)


def _const_spec(shape):
    nd = len(shape)
    return pl.BlockSpec(shape, lambda *_: (0,) * nd, pipeline_mode=pl.Buffered(1))


def _store_token_rows(ref, val):
    n, d = val.shape
    c_n = d // LANES
    for c in range(c_n):
        ref[pl.ds(c, n, stride=c_n), :] = val[:, c * LANES:(c + 1) * LANES]


def _load_token_rows(ref, n):
    c_n = ref.shape[0] // n
    return jnp.concatenate([ref[pl.ds(c, n, stride=c_n), :] for c in range(c_n)], axis=1)


def _inproj_kernel(x_ref, g_ref, w_ref, o_ref, h_ref, *, chunk):
    tm = x_ref.shape[0]

    @pl.when(pl.program_id(1) == 0)
    def _():
        def body(c, _):
            r0 = pl.multiple_of(c * chunk, chunk)
            x = x_ref[pl.ds(r0, chunk), :]
            ms = jnp.mean(x * x, axis=-1, keepdims=True)
            h_ref[pl.ds(r0, chunk), :] = (x * lax.rsqrt(ms + RMS_EPS) * g_ref[...]).astype(BF16)
            return 0
        lax.fori_loop(0, tm // chunk, body, 0)

    o_ref[...] = jnp.dot(h_ref[...], w_ref[...], preferred_element_type=F32).astype(o_ref.dtype)


def _inproj(x2, g, w_bf16, tm=1024, tn=1024):
    T, D = x2.shape
    N = w_bf16.shape[1]
    tm = min(tm, T)
    return pl.pallas_call(
        functools.partial(_inproj_kernel, chunk=min(256, tm)),
        grid=(T // tm, N // tn),
        in_specs=[
            pl.BlockSpec((tm, D), lambda i, j: (i, 0)),
            pl.BlockSpec((1, D), lambda i, j: (0, 0)),
            pl.BlockSpec((D, tn), lambda i, j: (0, j)),
        ],
        out_specs=pl.BlockSpec((tm, tn), lambda i, j: (i, j)),
        out_shape=jax.ShapeDtypeStruct((T, N), BF16),
        scratch_shapes=[pltpu.VMEM((tm, D), BF16)],
        compiler_params=_cparams(("parallel", "arbitrary")),
        name="inproj",
    )(x2, g, w_bf16)


def _expm1(y, u):
    near = (u - 1.0) * y / jnp.log(u)
    return jnp.where(u == 1.0, y, jnp.where(y > -0.5, near, u - 1.0))


def _rglru_kernel(xr_ref, gr_ref, cw_ref, cb_ref, wa_ref, ba_ref, wx_ref, bx_ref, lam_ref,
                  o_ref, xbuf_ref, hc_ref):
    ts = xr_ref.shape[1]
    W = xr_ref.shape[2]
    bd = W // LRU_BLOCKS
    pad = SUBLANES

    @pl.when(pl.program_id(1) == 0)
    def _():
        xbuf_ref[0:pad, :] = jnp.zeros((pad, W), F32)
        hc_ref[...] = jnp.zeros_like(hc_ref)

    xbuf_ref[pad:pad + ts, :] = xr_ref[0].astype(F32)
    xc = cb_ref[...] + cw_ref[0:1, :] * xbuf_ref[pl.ds(pad - 3, ts), :]
    for k in range(1, CONV_WIDTH):
        xc = xc + cw_ref[k:k + 1, :] * xbuf_ref[pl.ds(pad - 3 + k, ts), :]
    xbuf_ref[0:pad, :] = xbuf_ref[ts:ts + pad, :]

    xcb = xc.astype(BF16)
    ra, ri = [], []
    for n in range(LRU_BLOCKS):
        blk = xcb[:, n * bd:(n + 1) * bd]
        ra.append(jnp.dot(blk, wa_ref[n], preferred_element_type=F32))
        ri.append(jnp.dot(blk, wx_ref[n], preferred_element_type=F32))
    r = jax.nn.sigmoid(jnp.concatenate(ra, axis=-1) + ba_ref[...])
    i = jax.nn.sigmoid(jnp.concatenate(ri, axis=-1) + bx_ref[...])
    lam = lam_ref[...]
    log_sig = jnp.minimum(lam, 0.0) - jnp.log1p(jnp.exp(-jnp.abs(lam)))
    log_a = LRU_C * r * log_sig
    a = jnp.exp(log_a)
    b = jnp.sqrt(-_expm1(2.0 * log_a, a * a)) * (i * xc)

    row = lax.broadcasted_iota(jnp.int32, (ts, W), 0)
    d = 1
    while d < ts:
        keep = row >= d
        a_sh = jnp.where(keep, pltpu.roll(a, d, 0), 1.0)
        b_sh = jnp.where(keep, pltpu.roll(b, d, 0), 0.0)
        b = a * b_sh + b
        a = a * a_sh
        d *= 2
    h = b + a * hc_ref[...]
    hc_ref[...] = h[ts - 1:ts, :]

    o_ref[0] = (h * jax.nn.gelu(gr_ref[0].astype(F32), approximate=True)).astype(o_ref.dtype)


def _rglru(u3, conv_w, conv_b, wa, ba, wx, bx, lam, W, ts=256):
    B, S, _ = u3.shape
    ts = min(ts, S)
    vec = lambda: pl.BlockSpec((1, W), lambda b, s: (0, 0))
    bd = W // LRU_BLOCKS
    return pl.pallas_call(
        _rglru_kernel,
        grid=(B, S // ts),
        in_specs=[
            pl.BlockSpec((1, ts, W), lambda b, s: (b, s, 0)),
            pl.BlockSpec((1, ts, W), lambda b, s: (b, s, 1)),
            pl.BlockSpec((CONV_WIDTH, W), lambda b, s: (0, 0)),
            vec(),
            pl.BlockSpec((LRU_BLOCKS, bd, bd), lambda b, s: (0, 0, 0)),
            vec(),
            pl.BlockSpec((LRU_BLOCKS, bd, bd), lambda b, s: (0, 0, 0)),
            vec(),
            vec(),
        ],
        out_specs=pl.BlockSpec((1, ts, W), lambda b, s: (b, s, 0)),
        out_shape=jax.ShapeDtypeStruct((B, S, W), BF16),
        scratch_shapes=[pltpu.VMEM((SUBLANES + ts, W), F32), pltpu.VMEM((1, W), F32)],
        compiler_params=_cparams(("parallel", "arbitrary")),
        name="rglru",
    )(u3, u3, conv_w, conv_b, wa, ba, wx, bx, lam)


LOG2E = 1.4426950408889634
MOBA_GROUP = 4
MOBA_HEADS_PER_STEP = 2


def _split3(c):
    c0 = c.astype(BF16).astype(F32)
    r1 = c - c0
    c1 = r1.astype(BF16).astype(F32)
    c2 = (r1 - c1).astype(BF16).astype(F32)
    return c0, c1, c2


def _moba_kernel(slope_ref, q_ref, k_ref, v_ref, o_ref,
                 kaug_ref, vt_ref, kmean_ref, qaug_ref, causal_ref, m_ref, l_ref, acc_ref,
                 za_ref, zb_ref, *, G, HB):
    BLK, dh, EXT = MOBA_BLOCK, HEAD_DIM, LANES
    nb = k_ref.shape[1] // BLK
    j = pl.program_id(2)
    heads = range(HB)
    cols = lambda hh: slice(hh * dh, (hh + 1) * dh)

    @pl.when(j == 0)
    def _():
        lane = lax.broadcasted_iota(jnp.int32, (BLK, EXT), 1)
        rowi = lax.broadcasted_iota(jnp.int32, (BLK, EXT), 0)
        for hh in heads:
            slope = slope_ref[pl.program_id(1) * HB + hh]
            for n in range(nb):
                rows = slice(n * BLK, (n + 1) * BLK)
                kb = k_ref[0, rows, cols(hh)]
                kmean_ref[hh, n:n + 1, :] = jnp.mean(kb.astype(F32), axis=0, keepdims=True)
                c0, c1, c2 = _split3((LOG2E * slope) * (rowi + n * BLK).astype(F32))
                ext = jnp.where(lane == n, 1.0,
                                jnp.where(lane == nb, c0, jnp.where(lane == nb + 1, c1,
                                                                    jnp.where(lane == nb + 2, c2, 0.0))))
                kaug_ref[hh, rows, 0:dh] = kb
                kaug_ref[hh, rows, dh:dh + EXT] = ext.astype(BF16)
                vt_ref[hh, n // G, :, (n % G) * BLK:(n % G + 1) * BLK] = (
                    v_ref[0, rows, cols(hh)].astype(F32).T.astype(BF16))
        kk = lax.broadcasted_iota(jnp.int32, (BLK, BLK), 0)
        qq = lax.broadcasted_iota(jnp.int32, (BLK, BLK), 1)
        causal_ref[...] = jnp.where(kk <= qq, 0.0, NEG_BIG)

    blk_i = lax.broadcasted_iota(jnp.int32, (nb, BLK), 0)
    past = blk_i < j
    tail_row = lax.broadcasted_iota(jnp.int32, (EXT - nb, BLK), 0)
    ones3 = jnp.where(tail_row < 3, 1.0, 0.0)
    for hh in heads:
        qf = q_ref[0, :, cols(hh)].astype(F32)
        gate = lax.dot_general(kmean_ref[hh], qf, (((1,), (1,)), ((), ())),
                               preferred_element_type=F32,
                               precision=lax.Precision.HIGHEST)
        g = jnp.where(past, gate, -jnp.inf)
        sel = blk_i == j
        for _ in range(min(MOBA_TOPK, nb)):
            cmax = jnp.max(g, axis=0, keepdims=True)
            first = jnp.min(jnp.where(g == cmax, blk_i, nb), axis=0, keepdims=True)
            hit = blk_i == first
            sel = jnp.logical_or(sel, jnp.logical_and(hit, past))
            g = jnp.where(hit, -jnp.inf, g)
        mask = jnp.where(sel, 0.0, NEG_BIG)
        qaug_ref[hh] = jnp.concatenate([(qf * (HEAD_DIM ** -0.5 * LOG2E)).T, mask, ones3],
                                       axis=0).astype(BF16)

    def scores(hh, gi):
        kg = kaug_ref[hh, pl.ds(pl.multiple_of(gi * (G * BLK), G * BLK), G * BLK), :]
        return jnp.dot(kg, qaug_ref[hh], preferred_element_type=F32)

    def update(hh, gi, z):
        m_old = m_ref[hh]
        m_new = jnp.maximum(m_old, jnp.max(z, axis=0, keepdims=True))
        alpha = jnp.exp2(m_old - m_new)
        p = jnp.exp2(z - m_new)
        m_ref[hh] = m_new
        l_ref[hh] = alpha * l_ref[hh] + jnp.sum(p, axis=0, keepdims=True)
        acc_ref[hh] = alpha * acc_ref[hh] + jnp.dot(vt_ref[hh, gi], p.astype(BF16),
                                                    preferred_element_type=F32)

    g_own = j // G
    last = jnp.maximum(g_own - 1, 0)
    in_grp = lax.broadcasted_iota(jnp.int32, (G, BLK, BLK), 0)
    own_mask = jnp.where(in_grp == j % G, causal_ref[...][None], 0.0)
    z_own = [scores(hh, g_own) for hh in heads]
    for hh in heads:
        za_ref[hh] = scores(hh, 0)
    for hh in heads:
        z = (z_own[hh].reshape(G, BLK, BLK) + own_mask).reshape(G * BLK, BLK)
        m0 = jnp.max(z, axis=0, keepdims=True)
        p = jnp.exp2(z - m0)
        m_ref[hh] = m0
        l_ref[hh] = jnp.sum(p, axis=0, keepdims=True)
        acc_ref[hh] = jnp.dot(vt_ref[hh, g_own], p.astype(BF16), preferred_element_type=F32)

    def body(t, _):
        g0 = 2 * t
        for hh in heads:
            zb_ref[hh] = scores(hh, jnp.minimum(g0 + 1, last))
        for hh in heads:
            update(hh, g0, za_ref[hh])

        @pl.when(g0 + 1 < g_own)
        def _():
            for hh in heads:
                za_ref[hh] = scores(hh, jnp.minimum(g0 + 2, last))
            for hh in heads:
                update(hh, g0 + 1, zb_ref[hh])
        return 0

    lax.fori_loop(0, (g_own + 1) // 2, body, 0)
    for hh in heads:
        o_ref[0, :, cols(hh)] = (acc_ref[hh] / l_ref[hh]).T.astype(o_ref.dtype)


def _moba(u3, slopes, q_col, k_col, v_col):
    B, S, _ = u3.shape
    BLK, dh, H, EXT = MOBA_BLOCK, HEAD_DIM, ATTN_HEADS, LANES
    nb = S // BLK
    G = min(MOBA_GROUP, nb)
    HB = MOBA_HEADS_PER_STEP
    assert nb % G == 0 and nb % SUBLANES == 0 and nb + 3 <= EXT
    assert H % HB == 0 and q_col % HB == 0 and k_col % HB == 0 and v_col % HB == 0
    return pl.pallas_call(
        functools.partial(_moba_kernel, G=G, HB=HB),
        grid=(B, H // HB, nb),
        in_specs=[
            pl.BlockSpec(memory_space=pltpu.SMEM),
            pl.BlockSpec((1, BLK, HB * dh), lambda b, h, j: (b, j, q_col // HB + h)),
            pl.BlockSpec((1, S, HB * dh), lambda b, h, j: (b, 0, k_col // HB + h)),
            pl.BlockSpec((1, S, HB * dh), lambda b, h, j: (b, 0, v_col // HB + h)),
        ],
        out_specs=pl.BlockSpec((1, BLK, HB * dh), lambda b, h, j: (b, j, h)),
        out_shape=jax.ShapeDtypeStruct((B, S, H * dh), BF16),
        scratch_shapes=[
            pltpu.VMEM((HB, S, dh + EXT), BF16),
            pltpu.VMEM((HB, nb // G, dh, G * BLK), BF16),
            pltpu.VMEM((HB, nb, dh), F32),
            pltpu.VMEM((HB, dh + EXT, BLK), BF16),
            pltpu.VMEM((BLK, BLK), F32),
            pltpu.VMEM((HB, 1, BLK), F32),
            pltpu.VMEM((HB, 1, BLK), F32),
            pltpu.VMEM((HB, dh, BLK), F32),
            pltpu.VMEM((HB, G * BLK, BLK), F32),
            pltpu.VMEM((HB, G * BLK, BLK), F32),
        ],
        compiler_params=_cparams(("parallel", "parallel", "arbitrary")),
        name="moba",
    )(slopes, u3, u3, u3)


def _merge_kernel(yr_ref, ya_ref, gr0_ref, gr1_ref, ga0_ref, ga1_ref, x_ref,
                  pr_ref, pa_ref, wo_ref, g_ref, wr_ref, br_ref,
                  x1_ref, h2_ref, lg_ref):
    half = gr0_ref.shape[1]
    pr = jnp.dot(yr_ref[...], pr_ref[...], preferred_element_type=F32)
    pa = jnp.dot(ya_ref[...], pa_ref[...], preferred_element_type=F32)
    sig = lambda ref: jax.nn.sigmoid(ref[...].astype(F32))
    m0 = sig(gr0_ref) * pr[:, :half] + sig(ga0_ref) * pa[:, :half]
    m1 = sig(gr1_ref) * pr[:, half:] + sig(ga1_ref) * pa[:, half:]
    merged = jnp.concatenate([m0, m1], axis=-1).astype(BF16)
    x1 = x_ref[...] + jnp.dot(merged, wo_ref[...], preferred_element_type=F32)
    x1_ref[...] = x1
    ms = jnp.mean(x1 * x1, axis=-1, keepdims=True)
    h2 = x1 * lax.rsqrt(ms + RMS_EPS) * g_ref[...]
    _store_token_rows(h2_ref, h2)
    hi = h2.astype(BF16)
    lo = (h2 - hi.astype(F32)).astype(BF16)
    dot = functools.partial(jnp.dot, preferred_element_type=F32)
    lg_ref[...] = dot(hi, wr_ref[0]) + dot(lo, wr_ref[0]) + dot(hi, wr_ref[1]) + br_ref[...]


def _hi_lo(w):
    hi = w.astype(BF16)
    return jnp.stack([hi, (w - hi.astype(F32)).astype(BF16)])


def _merge(y_rec, y_attn, u, gate_col, x2, pr, pa, wo, g, wr, br, tm=256):
    T, D = x2.shape
    W = y_rec.shape[1]
    half = D // 2
    tm = min(tm, T)
    gspec = lambda c: pl.BlockSpec((tm, half), lambda i: (i, gate_col + c))
    row = lambda n: pl.BlockSpec((tm, n), lambda i: (i, 0))
    return pl.pallas_call(
        _merge_kernel,
        grid=(T // tm,),
        in_specs=[
            row(W), row(W), gspec(0), gspec(1), gspec(2), gspec(3), row(D),
            _const_spec(pr.shape), _const_spec(pa.shape), _const_spec(wo.shape),
            _const_spec(g.shape), _const_spec(wr.shape), _const_spec(br.shape),
        ],
        out_specs=[row(D), pl.BlockSpec((tm * (D // LANES), LANES), lambda i: (i, 0)), row(LANES)],
        out_shape=[
            jax.ShapeDtypeStruct((T, D), F32),
            jax.ShapeDtypeStruct((T * (D // LANES), LANES), F32),
            jax.ShapeDtypeStruct((T, LANES), F32),
        ],
        compiler_params=_cparams(("parallel",)),
        name="merge",
    )(y_rec, y_attn, u, u, u, u, x2, pr, pa, wo, g, wr, br)


def _route_kernel(lg_ref, o_ref):
    lg = lg_ref[...]
    G, EPG = N_GROUPS, EXPERTS_PER_GROUP
    lane = lax.broadcasted_iota(jnp.int32, lg.shape, 1)

    def first_argmax(vals, mask):
        v = jnp.where(mask, vals, -jnp.inf)
        mx = jnp.max(v, axis=-1, keepdims=True)
        idx = jnp.min(jnp.where(jnp.logical_and(mask, v == mx), lane, LANES), axis=-1, keepdims=True)
        return mx, idx

    gmask = lane < G
    gmax, gi = first_argmax(lg, gmask)
    gexp = jnp.where(gmask, jnp.exp(lg - gmax), 0.0)
    grp_w = 1.0 / jnp.sum(gexp, axis=-1, keepdims=True)
    lo = G + gi * EPG
    emask = jnp.logical_and(lane >= lo, lane < lo + EPG)
    emax, i1 = first_argmax(lg, emask)
    eexp = jnp.where(emask, jnp.exp(lg - emax), 0.0)
    esum = jnp.sum(eexp, axis=-1, keepdims=True)
    p1 = 1.0 / esum
    mask2 = jnp.logical_and(emask, lane != i1)
    l2, i2 = first_argmax(lg, mask2)
    p2 = jnp.exp(l2 - emax) / esum
    den = p1 + p2
    w1 = grp_w * (p1 / den)
    w2 = grp_w * (p2 / den)
    e1 = (i1 - G).astype(F32)
    e2 = (i2 - G).astype(F32)
    out = jnp.where(lane == 0, e1, jnp.where(lane == 1, e2, jnp.where(lane == 2, w1, jnp.where(lane == 3, w2, 0.0))))
    o_ref[...] = out


def _route(logits, tm=512):
    T = logits.shape[0]
    tm = min(tm, T)
    spec = pl.BlockSpec((tm, LANES), lambda i: (i, 0))
    return pl.pallas_call(
        _route_kernel,
        grid=(T // tm,),
        in_specs=[spec],
        out_specs=spec,
        out_shape=jax.ShapeDtypeStruct((T, LANES), F32),
        compiler_params=_cparams(("parallel",)),
        name="route",
    )(logits)


def _token_copy(src_hbm, src_tok, dst_ref, dst_tok, sem, c_n):
    return pltpu.make_async_copy(src_hbm.at[pl.ds(pl.multiple_of(src_tok * c_n, c_n), c_n), :],
                                 dst_ref.at[pl.ds(dst_tok * c_n, c_n), :], sem)


def _wait_tokens(src_hbm, dst_ref, sem):
    pltpu.make_async_copy(src_hbm.at[pl.ds(0, dst_ref.shape[0]), :], dst_ref, sem).wait()


def _ffn_kernel(te_ref, cur_ref, nxt_ref, h_hbm, wg_ref, wu_ref, wd_ref, o_ref,
                xbuf_ref, wgb_ref, wub_ref, wdb_ref, sem):
    i = pl.program_id(0)
    n = pl.num_programs(0)
    tr = cur_ref.shape[2]
    c_n = xbuf_ref.shape[1] // tr
    slot = i % 2
    parts = 4

    def gather(idx_ref, s, part):
        for r in range(part * tr // parts, (part + 1) * tr // parts):
            _token_copy(h_hbm, idx_ref[0, 0, r], xbuf_ref.at[s], r, sem.at[s], c_n).start(priority=r % 2)

    @pl.when(i == 0)
    def _():
        for part in range(parts):
            gather(cur_ref, 0, part)

    @pl.when(jnp.logical_or(i == 0, te_ref[i] != te_ref[jnp.maximum(i - 1, 0)]))
    def _():
        wgb_ref[...] = wg_ref[0].astype(BF16)
        wub_ref[...] = wu_ref[0].astype(BF16)
        wdb_ref[...] = wd_ref[0].astype(BF16)

    _wait_tokens(h_hbm, xbuf_ref.at[slot], sem.at[slot])
    x = _load_token_rows(xbuf_ref.at[slot], tr).astype(BF16)
    gather(nxt_ref, 1 - slot, 0)
    g = jnp.dot(x, wgb_ref[...], preferred_element_type=F32)
    gather(nxt_ref, 1 - slot, 1)
    u = jnp.dot(x, wub_ref[...], preferred_element_type=F32)
    hmid = (jax.nn.silu(g) * u).astype(BF16)
    gather(nxt_ref, 1 - slot, 2)
    y = jnp.dot(hmid, wdb_ref[...], preferred_element_type=F32)
    gather(nxt_ref, 1 - slot, 3)
    _store_token_rows(o_ref, y)

    @pl.when(i == n - 1)
    def _():
        _wait_tokens(h_hbm, xbuf_ref.at[1 - slot], sem.at[1 - slot])


def _expert_ffn(h2r, src_tok, tile_e, wg, wu, wd):
    D, Hd = wg.shape[1], wg.shape[2]
    c_n = D // LANES
    n_tiles = tile_e.shape[0]
    idx = src_tok.reshape(n_tiles, 1, ROW_TILE)
    grid_spec = pltpu.PrefetchScalarGridSpec(
        num_scalar_prefetch=1,
        grid=(n_tiles,),
        in_specs=[
            pl.BlockSpec((1, 1, ROW_TILE), lambda i, te: (i, 0, 0), memory_space=pltpu.SMEM),
            pl.BlockSpec((1, 1, ROW_TILE), lambda i, te: (jnp.minimum(i + 1, n_tiles - 1), 0, 0),
                         memory_space=pltpu.SMEM),
            pl.BlockSpec(memory_space=pl.ANY),
            pl.BlockSpec((1, D, Hd), lambda i, te: (te[i], 0, 0)),
            pl.BlockSpec((1, D, Hd), lambda i, te: (te[i], 0, 0)),
            pl.BlockSpec((1, Hd, D), lambda i, te: (te[i], 0, 0)),
        ],
        out_specs=pl.BlockSpec((ROW_TILE * c_n, LANES), lambda i, te: (i, 0)),
        scratch_shapes=[
            pltpu.VMEM((2, ROW_TILE * c_n, LANES), F32),
            pltpu.VMEM((D, Hd), BF16), pltpu.VMEM((D, Hd), BF16), pltpu.VMEM((Hd, D), BF16),
            pltpu.SemaphoreType.DMA((2,)),
        ],
    )
    return pl.pallas_call(
        _ffn_kernel,
        grid_spec=grid_spec,
        out_shape=jax.ShapeDtypeStruct((n_tiles * ROW_TILE * c_n, LANES), F32),
        compiler_params=_cparams(("arbitrary",)),
        name="expert_ffn",
    )(tile_e, idx, idx, h2r, wg, wu, wd)


def _combine_kernel(cur_ref, nxt_ref, x1_ref, rt_ref, g_ref, y_hbm, o_ref, buf_ref, sem):
    i = pl.program_id(0)
    n = pl.num_programs(0)
    tm = x1_ref.shape[0]
    c_n = buf_ref.shape[1] // (2 * tm)
    slot = i % 2

    def gather(pos_ref, s, r0, r1):
        for r in range(r0, r1):
            for k in range(2):
                _token_copy(y_hbm, pos_ref[0, 0, 2 * r + k], buf_ref.at[s], k * tm + r, sem.at[s],
                            c_n).start(priority=k)

    @pl.when(i == 0)
    def _():
        gather(cur_ref, 0, 0, tm)

    _wait_tokens(y_hbm, buf_ref.at[slot], sem.at[slot])
    gather(nxt_ref, 1 - slot, 0, tm // 2)
    y = _load_token_rows(buf_ref.at[slot], 2 * tm)
    rt = rt_ref[...]
    x = x1_ref[...] + rt[:, 2:3] * y[:tm] + rt[:, 3:4] * y[tm:]
    gather(nxt_ref, 1 - slot, tm // 2, tm)
    ms = jnp.mean(x * x, axis=-1, keepdims=True)
    o_ref[...] = x * lax.rsqrt(ms + RMS_EPS) * g_ref[...]

    @pl.when(i == n - 1)
    def _():
        _wait_tokens(y_hbm, buf_ref.at[1 - slot], sem.at[1 - slot])


def _combine(x1, routed, pos, g, y, tm=128):
    T, D = x1.shape
    tm = min(tm, T)
    c_n = D // LANES
    n = T // tm
    idx = pos.reshape(n, 1, 2 * tm)
    return pl.pallas_call(
        _combine_kernel,
        grid=(n,),
        in_specs=[
            pl.BlockSpec((1, 1, 2 * tm), lambda i: (i, 0, 0), memory_space=pltpu.SMEM),
            pl.BlockSpec((1, 1, 2 * tm), lambda i: (jnp.minimum(i + 1, n - 1), 0, 0),
                         memory_space=pltpu.SMEM),
            pl.BlockSpec((tm, D), lambda i: (i, 0)),
            pl.BlockSpec((tm, LANES), lambda i: (i, 0)),
            pl.BlockSpec((1, D), lambda i: (0, 0)),
            pl.BlockSpec(memory_space=pl.ANY),
        ],
        out_specs=pl.BlockSpec((tm, D), lambda i: (i, 0)),
        out_shape=jax.ShapeDtypeStruct((T, D), F32),
        scratch_shapes=[pltpu.VMEM((2, 2 * tm * c_n, LANES), F32), pltpu.SemaphoreType.DMA((2,))],
        compiler_params=_cparams(("arbitrary",)),
        name="combine",
    )(idx, idx, x1, routed, g, y)


def _sorted_layout(eid, n_tiles):
    E, TR = N_EXPERTS, ROW_TILE
    eflat = eid.reshape(-1)
    A = eflat.shape[0]
    onehot = (eflat[:, None] == jnp.arange(E, dtype=jnp.int32)[None, :]).astype(jnp.int32)
    csum = jnp.cumsum(onehot, axis=0)
    counts = csum[-1]
    ptiles = (counts + TR - 1) // TR
    tile_end = jnp.cumsum(ptiles)
    pad_off = (tile_end - ptiles) * TR
    pos = jnp.sum(onehot * (csum - 1 + pad_off[None, :]), axis=1)
    assert n_tiles * TR == A + E * TR
    k = jnp.arange(TR, dtype=jnp.int32)[None, :]
    n_pad = (ptiles * TR - counts)[:, None]
    pad_key = jnp.where(k < n_pad, (pad_off + counts)[:, None] + k, n_tiles * TR)
    keys = jnp.concatenate([pos, pad_key.reshape(-1)])
    toks = jnp.concatenate([jnp.arange(A, dtype=jnp.int32) // 2, jnp.zeros((E * TR,), jnp.int32)])
    _, src_tok = lax.sort_key_val(keys, toks)
    tiles = jnp.arange(n_tiles, dtype=jnp.int32)
    tile_e_raw = jnp.sum((tile_end[None, :] <= tiles[:, None]).astype(jnp.int32), axis=1)
    tile_e = jnp.minimum(tile_e_raw, E - 1)
    return src_tok, pos.astype(jnp.int32), tile_e


def kernel(x, norm_attn_g, w_in, conv_w, conv_b, lru_wa, lru_ba, lru_wx, lru_bx, lru_lambda, proj_rec, proj_attn, w_out, norm_ffn_g, router_group_w, router_group_b, router_expert_w, router_expert_b, expert_w_gate, expert_w_up, expert_w_down, norm_final_g):
    B, S, D = x.shape
    T = B * S
    depth = w_in.shape[0]
    W = lru_lambda.shape[1]
    AW = ATTN_HEADS * HEAD_DIM
    assert W % HEAD_DIM == 0 and W == AW and S % MOBA_BLOCK == 0
    q_col = 2 * W // HEAD_DIM
    k_col = q_col + ATTN_HEADS
    v_col = k_col + ATTN_HEADS
    gate_col = (2 * W + 3 * AW) // (D // 2)
    assert gate_col * (D // 2) == 2 * W + 3 * AW
    slopes = 2.0 ** (-8.0 * jnp.arange(1, ATTN_HEADS + 1, dtype=F32) / ATTN_HEADS)
    assert (2 * T) % ROW_TILE == 0
    n_tiles = 2 * T // ROW_TILE + N_EXPERTS
    row = lambda v: v.reshape(1, -1)

    assert depth == 1, "the final norm is fused into the single layer's combine"
    l = 0
    x2 = x.reshape(T, D)
    u = _inproj(x2, row(norm_attn_g[l]), w_in[l].astype(BF16))
    u3 = u.reshape(B, S, -1)
    y_rec = _rglru(u3, conv_w[l], row(conv_b[l]), lru_wa[l].astype(BF16), row(lru_ba[l]),
                   lru_wx[l].astype(BF16), row(lru_bx[l]), row(lru_lambda[l]), W)
    y_attn = _moba(u3, slopes, q_col, k_col, v_col)
    wr = jnp.zeros((D, LANES), F32)
    wr = wr.at[:, :N_GROUPS].set(router_group_w[l]).at[:, N_GROUPS:N_GROUPS + N_EXPERTS].set(router_expert_w[l])
    br = jnp.zeros((1, LANES), F32)
    br = br.at[0, :N_GROUPS].set(router_group_b[l]).at[0, N_GROUPS:N_GROUPS + N_EXPERTS].set(router_expert_b[l])
    x1, h2, logits = _merge(y_rec.reshape(T, W), y_attn.reshape(T, AW), u, gate_col, x2,
                            proj_rec[l].astype(BF16), proj_attn[l].astype(BF16),
                            w_out[l].astype(BF16), row(norm_ffn_g[l]), _hi_lo(wr), br)
    routed = _route(logits)
    eid = routed[:, :2].astype(jnp.int32)
    src_tok, pos, tile_e = _sorted_layout(eid, n_tiles)
    y = _expert_ffn(h2, src_tok, tile_e, expert_w_gate[l], expert_w_up[l], expert_w_down[l])
    out = _combine(x1, routed, pos, row(norm_final_g), y)
    return out.reshape(B, S, D)
```

```python
import functools

import jax
import jax.numpy as jnp
from jax import lax
from jax.experimental import pallas as pl
from jax.experimental.pallas import tpu as pltpu

F32 = jnp.float32
BF16 = jnp.bfloat16

RMS_EPS = 1e-6
LRU_BLOCKS = 8
CONV_WIDTH = 4
LRU_C = 8.0
ATTN_HEADS = 8
HEAD_DIM = 128
MOBA_BLOCK = 256
MOBA_TOPK = 3
N_GROUPS = 4
EXPERTS_PER_GROUP = 8
N_EXPERTS = N_GROUPS * EXPERTS_PER_GROUP

LANES = 128
SUBLANES = 8
NEG_BIG = -1e30
ROW_TILE = 256
VMEM_LIMIT = 56 * 1024 * 1024


def _cparams(sem):
    return pltpu.CompilerParams(dimension_semantics=sem, vmem_limit_bytes=VMEM_LIMIT)


def _const_spec(shape):
    nd = len(shape)
    return pl.BlockSpec(shape, lambda *_: (0,) * nd, pipeline_mode=pl.Buffered(1))


U32 = jnp.uint32


def _token_rows(d):
    return d // (2 * LANES)


def _store_token_rows(ref, val):
    n, d = val.shape
    c_n = d // (2 * LANES)
    bits = lambda v: lax.bitcast_convert_type(v.astype(BF16).astype(F32), U32)
    for c in range(c_n):
        lo = val[:, (2 * c) * LANES:(2 * c + 1) * LANES]
        hi = val[:, (2 * c + 1) * LANES:(2 * c + 2) * LANES]
        ref[pl.ds(c, n, stride=c_n), :] = jnp.bitwise_or(bits(hi), jnp.right_shift(bits(lo), 16))


def _load_token_rows(ref, n):
    c_n = ref.shape[0] // n
    cols = []
    for c in range(c_n):
        w = ref[pl.ds(c, n, stride=c_n), :]
        cols.append(lax.bitcast_convert_type(jnp.left_shift(w, 16), F32))
        cols.append(lax.bitcast_convert_type(jnp.bitwise_and(w, jnp.uint32(0xFFFF0000)), F32))
    return jnp.concatenate(cols, axis=1)


def _inproj_kernel(x_ref, g_ref, w_ref, o_ref, h_ref, *, chunk):
    tm = x_ref.shape[0]

    @pl.when(pl.program_id(1) == 0)
    def _():
        def body(c, _):
            r0 = pl.multiple_of(c * chunk, chunk)
            x = x_ref[pl.ds(r0, chunk), :]
            ms = jnp.mean(x * x, axis=-1, keepdims=True)
            h_ref[pl.ds(r0, chunk), :] = (x * lax.rsqrt(ms + RMS_EPS) * g_ref[...]).astype(BF16)
            return 0
        lax.fori_loop(0, tm // chunk, body, 0)

    o_ref[...] = jnp.dot(h_ref[...], w_ref[...], preferred_element_type=F32).astype(o_ref.dtype)


def _inproj(x2, g, w_bf16, tm=1024, tn=1024):
    T, D = x2.shape
    N = w_bf16.shape[1]
    tm = min(tm, T)
    return pl.pallas_call(
        functools.partial(_inproj_kernel, chunk=min(256, tm)),
        grid=(T // tm, N // tn),
        in_specs=[
            pl.BlockSpec((tm, D), lambda i, j: (i, 0)),
            pl.BlockSpec((1, D), lambda i, j: (0, 0)),
            pl.BlockSpec((D, tn), lambda i, j: (0, j)),
        ],
        out_specs=pl.BlockSpec((tm, tn), lambda i, j: (i, j)),
        out_shape=jax.ShapeDtypeStruct((T, N), BF16),
        scratch_shapes=[pltpu.VMEM((tm, D), BF16)],
        compiler_params=_cparams(("parallel", "arbitrary")),
        name="inproj",
    )(x2, g, w_bf16)


def _expm1(y, u):
    near = (u - 1.0) * y / jnp.log(u)
    return jnp.where(u == 1.0, y, jnp.where(y > -0.5, near, u - 1.0))


def _rglru_kernel(xr_ref, gr_ref, cw_ref, cb_ref, wa_ref, ba_ref, wx_ref, bx_ref, lam_ref,
                  o_ref, xbuf_ref, hc_ref):
    ts = xr_ref.shape[1]
    W = xr_ref.shape[2]
    bd = W // LRU_BLOCKS
    pad = SUBLANES

    @pl.when(pl.program_id(1) == 0)
    def _():
        xbuf_ref[0:pad, :] = jnp.zeros((pad, W), F32)
        hc_ref[...] = jnp.zeros_like(hc_ref)

    xbuf_ref[pad:pad + ts, :] = xr_ref[0].astype(F32)
    xc = cb_ref[...] + cw_ref[0:1, :] * xbuf_ref[pl.ds(pad - 3, ts), :]
    for k in range(1, CONV_WIDTH):
        xc = xc + cw_ref[k:k + 1, :] * xbuf_ref[pl.ds(pad - 3 + k, ts), :]
    xbuf_ref[0:pad, :] = xbuf_ref[ts:ts + pad, :]

    xcb = xc.astype(BF16)
    ra, ri = [], []
    for n in range(LRU_BLOCKS):
        blk = xcb[:, n * bd:(n + 1) * bd]
        ra.append(jnp.dot(blk, wa_ref[n], preferred_element_type=F32))
        ri.append(jnp.dot(blk, wx_ref[n], preferred_element_type=F32))
    r = jax.nn.sigmoid(jnp.concatenate(ra, axis=-1) + ba_ref[...])
    i = jax.nn.sigmoid(jnp.concatenate(ri, axis=-1) + bx_ref[...])
    lam = lam_ref[...]
    log_sig = jnp.minimum(lam, 0.0) - jnp.log1p(jnp.exp(-jnp.abs(lam)))
    log_a = LRU_C * r * log_sig
    a = jnp.exp(log_a)
    b = jnp.sqrt(-_expm1(2.0 * log_a, a * a)) * (i * xc)

    row = lax.broadcasted_iota(jnp.int32, (ts, W), 0)
    d = 1
    while d < ts:
        if d % SUBLANES:
            keep = row >= d
            a_sh = jnp.where(keep, pltpu.roll(a, d, 0), 1.0)
            b_sh = jnp.where(keep, pltpu.roll(b, d, 0), 0.0)
        else:
            a_sh = jnp.concatenate([jnp.ones((d, W), F32), a[:ts - d]], axis=0)
            b_sh = jnp.concatenate([jnp.zeros((d, W), F32), b[:ts - d]], axis=0)
        b = a * b_sh + b
        a = a * a_sh
        d *= 2
    h = b + a * hc_ref[...]
    hc_ref[...] = h[ts - 1:ts, :]

    o_ref[0] = (h * jax.nn.gelu(gr_ref[0].astype(F32), approximate=True)).astype(o_ref.dtype)


def _rglru(u3, conv_w, conv_b, wa, ba, wx, bx, lam, W, ts=256):
    B, S, _ = u3.shape
    ts = min(ts, S)
    vec = lambda: pl.BlockSpec((1, W), lambda b, s: (0, 0))
    bd = W // LRU_BLOCKS
    return pl.pallas_call(
        _rglru_kernel,
        grid=(B, S // ts),
        in_specs=[
            pl.BlockSpec((1, ts, W), lambda b, s: (b, s, 0)),
            pl.BlockSpec((1, ts, W), lambda b, s: (b, s, 1)),
            pl.BlockSpec((CONV_WIDTH, W), lambda b, s: (0, 0)),
            vec(),
            pl.BlockSpec((LRU_BLOCKS, bd, bd), lambda b, s: (0, 0, 0)),
            vec(),
            pl.BlockSpec((LRU_BLOCKS, bd, bd), lambda b, s: (0, 0, 0)),
            vec(),
            vec(),
        ],
        out_specs=pl.BlockSpec((1, ts, W), lambda b, s: (b, s, 0)),
        out_shape=jax.ShapeDtypeStruct((B, S, W), BF16),
        scratch_shapes=[pltpu.VMEM((SUBLANES + ts, W), F32), pltpu.VMEM((1, W), F32)],
        compiler_params=_cparams(("parallel", "arbitrary")),
        name="rglru",
    )(u3, u3, conv_w, conv_b, wa, ba, wx, bx, lam)


LOG2E = 1.4426950408889634
MOBA_GROUP = 4
MOBA_HEADS_PER_STEP = 2


def _split3(c):
    c0 = c.astype(BF16).astype(F32)
    r1 = c - c0
    c1 = r1.astype(BF16).astype(F32)
    c2 = (r1 - c1).astype(BF16).astype(F32)
    return c0, c1, c2


def _moba_kernel(slope_ref, q_ref, k_ref, v_ref, o_ref,
                 kaug_ref, vt_ref, kmean_ref, qaug_ref, causal_ref, m_ref, l_ref, acc_ref,
                 za_ref, zb_ref, *, G, HB):
    BLK, dh, EXT = MOBA_BLOCK, HEAD_DIM, LANES
    nb = k_ref.shape[1] // BLK
    j = pl.program_id(2)
    heads = range(HB)
    cols = lambda hh: slice(hh * dh, (hh + 1) * dh)

    @pl.when(j == 0)
    def _():
        lane = lax.broadcasted_iota(jnp.int32, (BLK, EXT), 1)
        rowi = lax.broadcasted_iota(jnp.int32, (BLK, EXT), 0)
        for hh in heads:
            slope = slope_ref[pl.program_id(1) * HB + hh]
            for n in range(nb):
                rows = slice(n * BLK, (n + 1) * BLK)
                kb = k_ref[0, rows, cols(hh)]
                kmean_ref[hh, n:n + 1, :] = jnp.mean(kb.astype(F32), axis=0, keepdims=True)
                c0, c1, c2 = _split3((LOG2E * slope) * (rowi + n * BLK).astype(F32))
                ext = jnp.where(lane == n, 1.0,
                                jnp.where(lane == nb, c0, jnp.where(lane == nb + 1, c1,
                                                                    jnp.where(lane == nb + 2, c2, 0.0))))
                kaug_ref[hh, rows, 0:dh] = kb
                kaug_ref[hh, rows, dh:dh + EXT] = ext.astype(BF16)
                vt_ref[hh, n // G, :, (n % G) * BLK:(n % G + 1) * BLK] = (
                    v_ref[0, rows, cols(hh)].astype(F32).T.astype(BF16))
        kk = lax.broadcasted_iota(jnp.int32, (BLK, BLK), 0)
        qq = lax.broadcasted_iota(jnp.int32, (BLK, BLK), 1)
        causal_ref[...] = jnp.where(kk <= qq, 0.0, NEG_BIG)

    blk_i = lax.broadcasted_iota(jnp.int32, (nb, BLK), 0)
    past = blk_i < j
    tail_row = lax.broadcasted_iota(jnp.int32, (EXT - nb, BLK), 0)
    ones3 = jnp.where(tail_row < 3, 1.0, 0.0)
    for hh in heads:
        qf = q_ref[0, :, cols(hh)].astype(F32)
        gate = lax.dot_general(kmean_ref[hh], qf, (((1,), (1,)), ((), ())),
                               preferred_element_type=F32,
                               precision=lax.Precision.HIGHEST)
        g = jnp.where(past, gate, -jnp.inf)
        sel = blk_i == j
        for _ in range(min(MOBA_TOPK, nb)):
            cmax = jnp.max(g, axis=0, keepdims=True)
            first = jnp.min(jnp.where(g == cmax, blk_i, nb), axis=0, keepdims=True)
            hit = blk_i == first
            sel = jnp.logical_or(sel, jnp.logical_and(hit, past))
            g = jnp.where(hit, -jnp.inf, g)
        mask = jnp.where(sel, 0.0, NEG_BIG)
        qaug_ref[hh] = jnp.concatenate([(qf * (HEAD_DIM ** -0.5 * LOG2E)).T, mask, ones3],
                                       axis=0).astype(BF16)

    def scores(hh, gi):
        kg = kaug_ref[hh, pl.ds(pl.multiple_of(gi * (G * BLK), G * BLK), G * BLK), :]
        return jnp.dot(kg, qaug_ref[hh], preferred_element_type=F32)

    def update(hh, gi, z):
        m_old = m_ref[hh]
        m_new = jnp.maximum(m_old, jnp.max(z, axis=0, keepdims=True))
        alpha = jnp.exp2(m_old - m_new)
        p = jnp.exp2(z - m_new)
        m_ref[hh] = m_new
        l_ref[hh] = alpha * l_ref[hh] + jnp.sum(p, axis=0, keepdims=True)
        acc_ref[hh] = alpha * acc_ref[hh] + jnp.dot(vt_ref[hh, gi], p.astype(BF16),
                                                    preferred_element_type=F32)

    g_own = j // G
    last = jnp.maximum(g_own - 1, 0)
    in_grp = lax.broadcasted_iota(jnp.int32, (G, BLK, BLK), 0)
    own_mask = jnp.where(in_grp == j % G, causal_ref[...][None], 0.0)
    z_own = [scores(hh, g_own) for hh in heads]
    for hh in heads:
        za_ref[hh] = scores(hh, 0)
    for hh in heads:
        z = (z_own[hh].reshape(G, BLK, BLK) + own_mask).reshape(G * BLK, BLK)
        m0 = jnp.max(z, axis=0, keepdims=True)
        p = jnp.exp2(z - m0)
        m_ref[hh] = m0
        l_ref[hh] = jnp.sum(p, axis=0, keepdims=True)
        acc_ref[hh] = jnp.dot(vt_ref[hh, g_own], p.astype(BF16), preferred_element_type=F32)

    def body(t, _):
        g0 = 2 * t
        for hh in heads:
            zb_ref[hh] = scores(hh, jnp.minimum(g0 + 1, last))
        for hh in heads:
            update(hh, g0, za_ref[hh])

        @pl.when(g0 + 1 < g_own)
        def _():
            for hh in heads:
                za_ref[hh] = scores(hh, jnp.minimum(g0 + 2, last))
            for hh in heads:
                update(hh, g0 + 1, zb_ref[hh])
        return 0

    lax.fori_loop(0, (g_own + 1) // 2, body, 0)
    for hh in heads:
        o_ref[0, :, cols(hh)] = (acc_ref[hh] / l_ref[hh]).T.astype(o_ref.dtype)


def _moba(u3, slopes, q_col, k_col, v_col):
    B, S, _ = u3.shape
    BLK, dh, H, EXT = MOBA_BLOCK, HEAD_DIM, ATTN_HEADS, LANES
    nb = S // BLK
    G = min(MOBA_GROUP, nb)
    HB = MOBA_HEADS_PER_STEP
    assert nb % G == 0 and nb % SUBLANES == 0 and nb + 3 <= EXT
    assert H % HB == 0 and q_col % HB == 0 and k_col % HB == 0 and v_col % HB == 0
    return pl.pallas_call(
        functools.partial(_moba_kernel, G=G, HB=HB),
        grid=(B, H // HB, nb),
        in_specs=[
            pl.BlockSpec(memory_space=pltpu.SMEM),
            pl.BlockSpec((1, BLK, HB * dh), lambda b, h, j: (b, j, q_col // HB + h)),
            pl.BlockSpec((1, S, HB * dh), lambda b, h, j: (b, 0, k_col // HB + h)),
            pl.BlockSpec((1, S, HB * dh), lambda b, h, j: (b, 0, v_col // HB + h)),
        ],
        out_specs=pl.BlockSpec((1, BLK, HB * dh), lambda b, h, j: (b, j, h)),
        out_shape=jax.ShapeDtypeStruct((B, S, H * dh), BF16),
        scratch_shapes=[
            pltpu.VMEM((HB, S, dh + EXT), BF16),
            pltpu.VMEM((HB, nb // G, dh, G * BLK), BF16),
            pltpu.VMEM((HB, nb, dh), F32),
            pltpu.VMEM((HB, dh + EXT, BLK), BF16),
            pltpu.VMEM((BLK, BLK), F32),
            pltpu.VMEM((HB, 1, BLK), F32),
            pltpu.VMEM((HB, 1, BLK), F32),
            pltpu.VMEM((HB, dh, BLK), F32),
            pltpu.VMEM((HB, G * BLK, BLK), F32),
            pltpu.VMEM((HB, G * BLK, BLK), F32),
        ],
        compiler_params=_cparams(("parallel", "parallel", "arbitrary")),
        name="moba",
    )(slopes, u3, u3, u3)


def _merge_kernel(yr_ref, ya_ref, gr0_ref, gr1_ref, ga0_ref, ga1_ref, x_ref,
                  pr_ref, pa_ref, wo_ref, g_ref, wr_ref, br_ref,
                  x1_ref, h2_ref, lg_ref):
    half = gr0_ref.shape[1]
    pr = jnp.dot(yr_ref[...], pr_ref[...], preferred_element_type=F32)
    pa = jnp.dot(ya_ref[...], pa_ref[...], preferred_element_type=F32)
    sig = lambda ref: jax.nn.sigmoid(ref[...].astype(F32))
    m0 = sig(gr0_ref) * pr[:, :half] + sig(ga0_ref) * pa[:, :half]
    m1 = sig(gr1_ref) * pr[:, half:] + sig(ga1_ref) * pa[:, half:]
    merged = jnp.concatenate([m0, m1], axis=-1).astype(BF16)
    x1 = x_ref[...] + jnp.dot(merged, wo_ref[...], preferred_element_type=F32)
    x1_ref[...] = x1
    ms = jnp.mean(x1 * x1, axis=-1, keepdims=True)
    h2 = x1 * lax.rsqrt(ms + RMS_EPS) * g_ref[...]
    _store_token_rows(h2_ref, h2)
    hi = h2.astype(BF16)
    lo = (h2 - hi.astype(F32)).astype(BF16)
    dot = functools.partial(jnp.dot, preferred_element_type=F32)
    lg_ref[...] = dot(hi, wr_ref[0]) + dot(lo, wr_ref[0]) + dot(hi, wr_ref[1]) + br_ref[...]


def _hi_lo(w):
    hi = w.astype(BF16)
    return jnp.stack([hi, (w - hi.astype(F32)).astype(BF16)])


def _merge(y_rec, y_attn, u, gate_col, x2, pr, pa, wo, g, wr, br, tm=256):
    T, D = x2.shape
    W = y_rec.shape[1]
    half = D // 2
    tm = min(tm, T)
    gspec = lambda c: pl.BlockSpec((tm, half), lambda i: (i, gate_col + c))
    row = lambda n: pl.BlockSpec((tm, n), lambda i: (i, 0))
    return pl.pallas_call(
        _merge_kernel,
        grid=(T // tm,),
        in_specs=[
            row(W), row(W), gspec(0), gspec(1), gspec(2), gspec(3), row(D),
            _const_spec(pr.shape), _const_spec(pa.shape), _const_spec(wo.shape),
            _const_spec(g.shape), _const_spec(wr.shape), _const_spec(br.shape),
        ],
        out_specs=[row(D), pl.BlockSpec((tm * _token_rows(D), LANES), lambda i: (i, 0)), row(LANES)],
        out_shape=[
            jax.ShapeDtypeStruct((T, D), F32),
            jax.ShapeDtypeStruct((T * _token_rows(D), LANES), U32),
            jax.ShapeDtypeStruct((T, LANES), F32),
        ],
        compiler_params=_cparams(("parallel",)),
        name="merge",
    )(y_rec, y_attn, u, u, u, u, x2, pr, pa, wo, g, wr, br)


def _route_kernel(lg_ref, o_ref):
    lg = lg_ref[...]
    G, EPG = N_GROUPS, EXPERTS_PER_GROUP
    lane = lax.broadcasted_iota(jnp.int32, lg.shape, 1)

    def first_argmax(vals, mask):
        v = jnp.where(mask, vals, -jnp.inf)
        mx = jnp.max(v, axis=-1, keepdims=True)
        idx = jnp.min(jnp.where(jnp.logical_and(mask, v == mx), lane, LANES), axis=-1, keepdims=True)
        return mx, idx

    gmask = lane < G
    gmax, gi = first_argmax(lg, gmask)
    gexp = jnp.where(gmask, jnp.exp(lg - gmax), 0.0)
    grp_w = 1.0 / jnp.sum(gexp, axis=-1, keepdims=True)
    lo = G + gi * EPG
    emask = jnp.logical_and(lane >= lo, lane < lo + EPG)
    emax, i1 = first_argmax(lg, emask)
    eexp = jnp.where(emask, jnp.exp(lg - emax), 0.0)
    esum = jnp.sum(eexp, axis=-1, keepdims=True)
    p1 = 1.0 / esum
    mask2 = jnp.logical_and(emask, lane != i1)
    l2, i2 = first_argmax(lg, mask2)
    p2 = jnp.exp(l2 - emax) / esum
    den = p1 + p2
    w1 = grp_w * (p1 / den)
    w2 = grp_w * (p2 / den)
    e1 = (i1 - G).astype(F32)
    e2 = (i2 - G).astype(F32)
    out = jnp.where(lane == 0, e1, jnp.where(lane == 1, e2, jnp.where(lane == 2, w1, jnp.where(lane == 3, w2, 0.0))))
    o_ref[...] = out


def _route(logits, tm=512):
    T = logits.shape[0]
    tm = min(tm, T)
    spec = pl.BlockSpec((tm, LANES), lambda i: (i, 0))
    return pl.pallas_call(
        _route_kernel,
        grid=(T // tm,),
        in_specs=[spec],
        out_specs=spec,
        out_shape=jax.ShapeDtypeStruct((T, LANES), F32),
        compiler_params=_cparams(("parallel",)),
        name="route",
    )(logits)


def _token_copy(src_hbm, src_tok, dst_ref, dst_tok, sem, c_n):
    return pltpu.make_async_copy(src_hbm.at[pl.ds(pl.multiple_of(src_tok * c_n, c_n), c_n), :],
                                 dst_ref.at[pl.ds(dst_tok * c_n, c_n), :], sem)


def _wait_tokens(src_hbm, dst_ref, sem):
    pltpu.make_async_copy(src_hbm.at[pl.ds(0, dst_ref.shape[0]), :], dst_ref, sem).wait()


def _ffn_kernel(te_ref, cur_ref, nxt_ref, h_hbm, wg_ref, wu_ref, wd_ref, o_ref,
                xbuf_ref, wgb_ref, wub_ref, wdb_ref, sem):
    i = pl.program_id(0)
    n = pl.num_programs(0)
    tr = cur_ref.shape[2]
    c_n = xbuf_ref.shape[1] // tr
    slot = i % 2
    parts = 4

    def gather(idx_ref, s, part):
        for r in range(part * tr // parts, (part + 1) * tr // parts):
            _token_copy(h_hbm, idx_ref[0, 0, r], xbuf_ref.at[s], r, sem.at[s], c_n).start(priority=r % 2)

    @pl.when(i == 0)
    def _():
        for part in range(parts):
            gather(cur_ref, 0, part)

    @pl.when(jnp.logical_or(i == 0, te_ref[i] != te_ref[jnp.maximum(i - 1, 0)]))
    def _():
        wgb_ref[...] = wg_ref[0].astype(BF16)
        wub_ref[...] = wu_ref[0].astype(BF16)
        wdb_ref[...] = wd_ref[0].astype(BF16)

    _wait_tokens(h_hbm, xbuf_ref.at[slot], sem.at[slot])
    x = _load_token_rows(xbuf_ref.at[slot], tr).astype(BF16)
    gather(nxt_ref, 1 - slot, 0)
    g = jnp.dot(x, wgb_ref[...], preferred_element_type=F32)
    gather(nxt_ref, 1 - slot, 1)
    u = jnp.dot(x, wub_ref[...], preferred_element_type=F32)
    hmid = (jax.nn.silu(g) * u).astype(BF16)
    gather(nxt_ref, 1 - slot, 2)
    y = jnp.dot(hmid, wdb_ref[...], preferred_element_type=F32)
    gather(nxt_ref, 1 - slot, 3)
    _store_token_rows(o_ref, y)

    @pl.when(i == n - 1)
    def _():
        _wait_tokens(h_hbm, xbuf_ref.at[1 - slot], sem.at[1 - slot])


def _expert_ffn(h2r, src_tok, tile_e, wg, wu, wd):
    D, Hd = wg.shape[1], wg.shape[2]
    c_n = _token_rows(D)
    n_tiles = tile_e.shape[0]
    idx = src_tok.reshape(n_tiles, 1, ROW_TILE)
    grid_spec = pltpu.PrefetchScalarGridSpec(
        num_scalar_prefetch=1,
        grid=(n_tiles,),
        in_specs=[
            pl.BlockSpec((1, 1, ROW_TILE), lambda i, te: (i, 0, 0), memory_space=pltpu.SMEM),
            pl.BlockSpec((1, 1, ROW_TILE), lambda i, te: (jnp.minimum(i + 1, n_tiles - 1), 0, 0),
                         memory_space=pltpu.SMEM),
            pl.BlockSpec(memory_space=pl.ANY),
            pl.BlockSpec((1, D, Hd), lambda i, te: (te[i], 0, 0)),
            pl.BlockSpec((1, D, Hd), lambda i, te: (te[i], 0, 0)),
            pl.BlockSpec((1, Hd, D), lambda i, te: (te[i], 0, 0)),
        ],
        out_specs=pl.BlockSpec((ROW_TILE * c_n, LANES), lambda i, te: (i, 0)),
        scratch_shapes=[
            pltpu.VMEM((2, ROW_TILE * c_n, LANES), U32),
            pltpu.VMEM((D, Hd), BF16), pltpu.VMEM((D, Hd), BF16), pltpu.VMEM((Hd, D), BF16),
            pltpu.SemaphoreType.DMA((2,)),
        ],
    )
    return pl.pallas_call(
        _ffn_kernel,
        grid_spec=grid_spec,
        out_shape=jax.ShapeDtypeStruct((n_tiles * ROW_TILE * c_n, LANES), U32),
        compiler_params=_cparams(("arbitrary",)),
        name="expert_ffn",
    )(tile_e, idx, idx, h2r, wg, wu, wd)


def _combine_kernel(cur_ref, nxt_ref, x1_ref, rt_ref, g_ref, y_hbm, o_ref, buf_ref, sem):
    i = pl.program_id(0)
    n = pl.num_programs(0)
    tm = x1_ref.shape[0]
    c_n = buf_ref.shape[1] // (2 * tm)
    slot = i % 2

    def gather(pos_ref, s):
        for r in range(tm):
            for k in range(2):
                _token_copy(y_hbm, pos_ref[0, 0, 2 * r + k], buf_ref.at[s], k * tm + r, sem.at[s], c_n).start()

    @pl.when(i == 0)
    def _():
        gather(cur_ref, 0)

    _wait_tokens(y_hbm, buf_ref.at[slot], sem.at[slot])
    gather(nxt_ref, 1 - slot)
    y = _load_token_rows(buf_ref.at[slot], 2 * tm)
    rt = rt_ref[...]
    x = x1_ref[...] + rt[:, 2:3] * y[:tm] + rt[:, 3:4] * y[tm:]
    ms = jnp.mean(x * x, axis=-1, keepdims=True)
    o_ref[...] = x * lax.rsqrt(ms + RMS_EPS) * g_ref[...]

    @pl.when(i == n - 1)
    def _():
        _wait_tokens(y_hbm, buf_ref.at[1 - slot], sem.at[1 - slot])


def _combine(x1, routed, pos, g, y, tm=128):
    T, D = x1.shape
    tm = min(tm, T)
    c_n = _token_rows(D)
    n = T // tm
    idx = pos.reshape(n, 1, 2 * tm)
    return pl.pallas_call(
        _combine_kernel,
        grid=(n,),
        in_specs=[
            pl.BlockSpec((1, 1, 2 * tm), lambda i: (i, 0, 0), memory_space=pltpu.SMEM),
            pl.BlockSpec((1, 1, 2 * tm), lambda i: (jnp.minimum(i + 1, n - 1), 0, 0),
                         memory_space=pltpu.SMEM),
            pl.BlockSpec((tm, D), lambda i: (i, 0)),
            pl.BlockSpec((tm, LANES), lambda i: (i, 0)),
            pl.BlockSpec((1, D), lambda i: (0, 0)),
            pl.BlockSpec(memory_space=pl.ANY),
        ],
        out_specs=pl.BlockSpec((tm, D), lambda i: (i, 0)),
        out_shape=jax.ShapeDtypeStruct((T, D), F32),
        scratch_shapes=[pltpu.VMEM((2, 2 * tm * c_n, LANES), U32), pltpu.SemaphoreType.DMA((2,))],
        compiler_params=_cparams(("arbitrary",)),
        name="combine",
    )(idx, idx, x1, routed, g, y)


def _sorted_layout(eid, n_tiles):
    E, TR = N_EXPERTS, ROW_TILE
    eflat = eid.reshape(-1)
    A = eflat.shape[0]
    onehot = (eflat[:, None] == jnp.arange(E, dtype=jnp.int32)[None, :]).astype(jnp.int32)
    csum = jnp.cumsum(onehot, axis=0)
    counts = csum[-1]
    ptiles = (counts + TR - 1) // TR
    tile_end = jnp.cumsum(ptiles)
    pad_off = (tile_end - ptiles) * TR
    pos = jnp.sum(onehot * (csum - 1 + pad_off[None, :]), axis=1)
    assert n_tiles * TR == A + E * TR
    k = jnp.arange(TR, dtype=jnp.int32)[None, :]
    n_pad = (ptiles * TR - counts)[:, None]
    pad_key = jnp.where(k < n_pad, (pad_off + counts)[:, None] + k, n_tiles * TR)
    keys = jnp.concatenate([pos, pad_key.reshape(-1)])
    toks = jnp.concatenate([jnp.arange(A, dtype=jnp.int32) // 2, jnp.zeros((E * TR,), jnp.int32)])
    _, src_tok = lax.sort_key_val(keys, toks)
    tiles = jnp.arange(n_tiles, dtype=jnp.int32)
    tile_e_raw = jnp.sum((tile_end[None, :] <= tiles[:, None]).astype(jnp.int32), axis=1)
    tile_e = jnp.minimum(tile_e_raw, E - 1)
    return src_tok, pos.astype(jnp.int32), tile_e


def kernel(x, norm_attn_g, w_in, conv_w, conv_b, lru_wa, lru_ba, lru_wx, lru_bx, lru_lambda, proj_rec, proj_attn, w_out, norm_ffn_g, router_group_w, router_group_b, router_expert_w, router_expert_b, expert_w_gate, expert_w_up, expert_w_down, norm_final_g):
    B, S, D = x.shape
    T = B * S
    depth = w_in.shape[0]
    W = lru_lambda.shape[1]
    AW = ATTN_HEADS * HEAD_DIM
    assert W % HEAD_DIM == 0 and W == AW and S % MOBA_BLOCK == 0
    q_col = 2 * W // HEAD_DIM
    k_col = q_col + ATTN_HEADS
    v_col = k_col + ATTN_HEADS
    gate_col = (2 * W + 3 * AW) // (D // 2)
    assert gate_col * (D // 2) == 2 * W + 3 * AW
    slopes = 2.0 ** (-8.0 * jnp.arange(1, ATTN_HEADS + 1, dtype=F32) / ATTN_HEADS)
    assert (2 * T) % ROW_TILE == 0
    n_tiles = 2 * T // ROW_TILE + N_EXPERTS
    row = lambda v: v.reshape(1, -1)

    assert depth == 1, "the final norm is fused into the single layer's combine"
    l = 0
    x2 = x.reshape(T, D)
    u = _inproj(x2, row(norm_attn_g[l]), w_in[l].astype(BF16))
    u3 = u.reshape(B, S, -1)
    y_rec = _rglru(u3, conv_w[l], row(conv_b[l]), lru_wa[l].astype(BF16), row(lru_ba[l]),
                   lru_wx[l].astype(BF16), row(lru_bx[l]), row(lru_lambda[l]), W)
    y_attn = _moba(u3, slopes, q_col, k_col, v_col)
    wr = jnp.zeros((D, LANES), F32)
    wr = wr.at[:, :N_GROUPS].set(router_group_w[l]).at[:, N_GROUPS:N_GROUPS + N_EXPERTS].set(router_expert_w[l])
    br = jnp.zeros((1, LANES), F32)
    br = br.at[0, :N_GROUPS].set(router_group_b[l]).at[0, N_GROUPS:N_GROUPS + N_EXPERTS].set(router_expert_b[l])
    x1, h2, logits = _merge(y_rec.reshape(T, W), y_attn.reshape(T, AW), u, gate_col, x2,
                            proj_rec[l].astype(BF16), proj_attn[l].astype(BF16),
                            w_out[l].astype(BF16), row(norm_ffn_g[l]), _hi_lo(wr), br)
    routed = _route(logits)
    eid = routed[:, :2].astype(jnp.int32)
    src_tok, pos, tile_e = _sorted_layout(eid, n_tiles)
    y = _expert_ffn(h2, src_tok, tile_e, expert_w_gate[l], expert_w_up[l], expert_w_down[l])
    out = _combine(x1, routed, pos, row(norm_final_g), y)
    return out.reshape(B, S, D)
```

```python
import functools

import jax
import jax.numpy as jnp
from jax import lax
from jax.experimental import pallas as pl
from jax.experimental.pallas import tpu as pltpu

F32 = jnp.float32
BF16 = jnp.bfloat16

RMS_EPS = 1e-6
LRU_BLOCKS = 8
CONV_WIDTH = 4
LRU_C = 8.0
ATTN_HEADS = 8
HEAD_DIM = 128
MOBA_BLOCK = 256
MOBA_TOPK = 3
N_GROUPS = 4
EXPERTS_PER_GROUP = 8
N_EXPERTS = N_GROUPS * EXPERTS_PER_GROUP

LANES = 128
SUBLANES = 8
NEG_BIG = -1e30
ROW_TILE = 256
VMEM_LIMIT = 56 * 1024 * 1024


def _cparams(sem):
    return pltpu.CompilerParams(dimension_semantics=sem, vmem_limit_bytes=VMEM_LIMIT)


def _const_spec(shape):
    nd = len(shape)
    return pl.BlockSpec(shape, lambda *_: (0,) * nd, pipeline_mode=pl.Buffered(1))


TOKEN_DTYPE = F32


def _token_rows(d):
    return d // LANES


def _store_token_rows(ref, val):
    n, d = val.shape
    c_n = _token_rows(d)
    for c in range(c_n):
        ref[pl.ds(c, n, stride=c_n), :] = val[:, c * LANES:(c + 1) * LANES]


def _load_token_rows(ref, n):
    c_n = ref.shape[0] // n
    return jnp.concatenate([ref[pl.ds(c, n, stride=c_n), :] for c in range(c_n)], axis=1)


def _inproj_kernel(x_ref, g_ref, w_ref, o_ref, h_ref, *, chunk):
    tm = x_ref.shape[0]

    @pl.when(pl.program_id(1) == 0)
    def _():
        def body(c, _):
            r0 = pl.multiple_of(c * chunk, chunk)
            x = x_ref[pl.ds(r0, chunk), :]
            ms = jnp.mean(x * x, axis=-1, keepdims=True)
            h_ref[pl.ds(r0, chunk), :] = (x * lax.rsqrt(ms + RMS_EPS) * g_ref[...]).astype(BF16)
            return 0
        lax.fori_loop(0, tm // chunk, body, 0)

    o_ref[...] = jnp.dot(h_ref[...], w_ref[...], preferred_element_type=F32).astype(o_ref.dtype)


def _inproj(x2, g, w_bf16, tm=1024, tn=1024):
    T, D = x2.shape
    N = w_bf16.shape[1]
    tm = min(tm, T)
    return pl.pallas_call(
        functools.partial(_inproj_kernel, chunk=min(256, tm)),
        grid=(T // tm, N // tn),
        in_specs=[
            pl.BlockSpec((tm, D), lambda i, j: (i, 0)),
            pl.BlockSpec((1, D), lambda i, j: (0, 0)),
            pl.BlockSpec((D, tn), lambda i, j: (0, j)),
        ],
        out_specs=pl.BlockSpec((tm, tn), lambda i, j: (i, j)),
        out_shape=jax.ShapeDtypeStruct((T, N), BF16),
        scratch_shapes=[pltpu.VMEM((tm, D), BF16)],
        compiler_params=_cparams(("parallel", "arbitrary")),
        name="inproj",
    )(x2, g, w_bf16)


def _rglru_kernel(xr_ref, gr_ref, cw_ref, cb_ref, wa_ref, ba_ref, wx_ref, bx_ref, lam_ref,
                  o_ref, xbuf_ref, hc_ref):
    ts = xr_ref.shape[1]
    W = xr_ref.shape[2]
    bd = W // LRU_BLOCKS
    pad = SUBLANES

    @pl.when(pl.program_id(1) == 0)
    def _():
        xbuf_ref[0:pad, :] = jnp.zeros((pad, W), F32)
        hc_ref[...] = jnp.zeros_like(hc_ref)

    xbuf_ref[pad:pad + ts, :] = xr_ref[0].astype(F32)
    xc = cb_ref[...] + cw_ref[0:1, :] * xbuf_ref[pl.ds(pad - 3, ts), :]
    for k in range(1, CONV_WIDTH):
        xc = xc + cw_ref[k:k + 1, :] * xbuf_ref[pl.ds(pad - 3 + k, ts), :]
    xbuf_ref[0:pad, :] = xbuf_ref[ts:ts + pad, :]

    xcb = xc.astype(BF16)
    ra, ri = [], []
    for n in range(LRU_BLOCKS):
        blk = xcb[:, n * bd:(n + 1) * bd]
        ra.append(jnp.dot(blk, wa_ref[n], preferred_element_type=F32))
        ri.append(jnp.dot(blk, wx_ref[n], preferred_element_type=F32))
    r = jax.nn.sigmoid(jnp.concatenate(ra, axis=-1) + ba_ref[...])
    i = jax.nn.sigmoid(jnp.concatenate(ri, axis=-1) + bx_ref[...])
    lam = lam_ref[...]
    log_sig = jnp.minimum(lam, 0.0) - jnp.log1p(jnp.exp(-jnp.abs(lam)))
    log_a = LRU_C * r * log_sig
    a = jnp.exp(log_a)
    th = jnp.tanh(-log_a)
    b = jnp.sqrt(2.0 * th / (1.0 + th)) * (i * xc)

    row = lax.broadcasted_iota(jnp.int32, (ts, W), 0)
    d = 1
    while d < ts:
        if d % SUBLANES:
            keep = row >= d
            a_sh = jnp.where(keep, pltpu.roll(a, d, 0), 1.0)
            b_sh = jnp.where(keep, pltpu.roll(b, d, 0), 0.0)
        else:
            a_sh = jnp.concatenate([jnp.ones((d, W), F32), a[:ts - d]], axis=0)
            b_sh = jnp.concatenate([jnp.zeros((d, W), F32), b[:ts - d]], axis=0)
        b = a * b_sh + b
        a = a * a_sh
        d *= 2
    h = b + a * hc_ref[...]
    hc_ref[...] = h[ts - 1:ts, :]

    o_ref[0] = (h * jax.nn.gelu(gr_ref[0].astype(F32), approximate=True)).astype(o_ref.dtype)


def _rglru(u3, conv_w, conv_b, wa, ba, wx, bx, lam, W, ts=256):
    B, S, _ = u3.shape
    ts = min(ts, S)
    vec = lambda: pl.BlockSpec((1, W), lambda b, s: (0, 0))
    bd = W // LRU_BLOCKS
    return pl.pallas_call(
        _rglru_kernel,
        grid=(B, S // ts),
        in_specs=[
            pl.BlockSpec((1, ts, W), lambda b, s: (b, s, 0)),
            pl.BlockSpec((1, ts, W), lambda b, s: (b, s, 1)),
            pl.BlockSpec((CONV_WIDTH, W), lambda b, s: (0, 0)),
            vec(),
            pl.BlockSpec((LRU_BLOCKS, bd, bd), lambda b, s: (0, 0, 0)),
            vec(),
            pl.BlockSpec((LRU_BLOCKS, bd, bd), lambda b, s: (0, 0, 0)),
            vec(),
            vec(),
        ],
        out_specs=pl.BlockSpec((1, ts, W), lambda b, s: (b, s, 0)),
        out_shape=jax.ShapeDtypeStruct((B, S, W), BF16),
        scratch_shapes=[pltpu.VMEM((SUBLANES + ts, W), F32), pltpu.VMEM((1, W), F32)],
        compiler_params=_cparams(("parallel", "arbitrary")),
        name="rglru",
    )(u3, u3, conv_w, conv_b, wa, ba, wx, bx, lam)


LOG2E = 1.4426950408889634
MOBA_GROUP = 4
MOBA_HEADS_PER_STEP = 2
MOBA_GATE_CHUNK = 8


def _split3(c):
    c0 = c.astype(BF16).astype(F32)
    r1 = c - c0
    c1 = r1.astype(BF16).astype(F32)
    c2 = (r1 - c1).astype(BF16).astype(F32)
    return c0, c1, c2


def _moba_kernel(slope_ref, q_ref, k_ref, v_ref, o_ref,
                 kaug_ref, vt_ref, kmean_ref, mask_ref, qaug_ref, causal_ref, m_ref, l_ref, acc_ref,
                 za_ref, zb_ref, *, G, HB):
    BLK, dh, EXT = MOBA_BLOCK, HEAD_DIM, LANES
    nb = k_ref.shape[1] // BLK
    j = pl.program_id(2)
    heads = range(HB)
    cols = lambda hh: slice(hh * dh, (hh + 1) * dh)

    @pl.when(j == 0)
    def _():
        lane = lax.broadcasted_iota(jnp.int32, (BLK, EXT), 1)
        rowi = lax.broadcasted_iota(jnp.int32, (BLK, EXT), 0)
        for hh in heads:
            slope = slope_ref[pl.program_id(1) * HB + hh]
            for n in range(nb):
                rows = slice(n * BLK, (n + 1) * BLK)
                kb = k_ref[0, rows, cols(hh)]
                kmean_ref[hh, n:n + 1, :] = jnp.mean(kb.astype(F32), axis=0, keepdims=True)
                c0, c1, c2 = _split3((LOG2E * slope) * (rowi + n * BLK).astype(F32))
                ext = jnp.where(lane == n, 1.0,
                                jnp.where(lane == nb, c0, jnp.where(lane == nb + 1, c1,
                                                                    jnp.where(lane == nb + 2, c2, 0.0))))
                kaug_ref[hh, rows, 0:dh] = kb
                kaug_ref[hh, rows, dh:dh + EXT] = ext.astype(BF16)
                vt_ref[hh, n // G, :, (n % G) * BLK:(n % G + 1) * BLK] = (
                    v_ref[0, rows, cols(hh)].astype(F32).T.astype(BF16))
        kk = lax.broadcasted_iota(jnp.int32, (BLK, BLK), 0)
        qq = lax.broadcasted_iota(jnp.int32, (BLK, BLK), 1)
        causal_ref[...] = jnp.where(kk <= qq, 0.0, NEG_BIG)

        qc = min(MOBA_GATE_CHUNK, nb) * BLK
        blk_i = lax.broadcasted_iota(jnp.int32, (nb, qc), 0)
        lane_i = lax.broadcasted_iota(jnp.int32, (nb, qc), 1)
        for hh in heads:
            for c0 in range(0, nb * BLK, qc):
                qa = q_ref[0, c0:c0 + qc, cols(hh)].astype(F32)
                gate = lax.dot_general(kmean_ref[hh], qa, (((1,), (1,)), ((), ())),
                                       preferred_element_type=F32,
                                       precision=lax.Precision.HIGHEST)
                qpos = lane_i + c0
                past = (blk_i + 1) * BLK <= qpos
                g = jnp.where(past, gate, -jnp.inf)
                sel = jnp.logical_and(blk_i * BLK <= qpos, jnp.logical_not(past))
                for _ in range(min(MOBA_TOPK, nb)):
                    cmax = jnp.max(g, axis=0, keepdims=True)
                    first = jnp.min(jnp.where(g == cmax, blk_i, nb), axis=0, keepdims=True)
                    hit = blk_i == first
                    sel = jnp.logical_or(sel, jnp.logical_and(hit, past))
                    g = jnp.where(hit, -jnp.inf, g)
                mask = jnp.where(sel, 0.0, NEG_BIG)
                for t in range(qc // BLK):
                    mask_ref[hh, c0 // BLK + t] = mask[:, t * BLK:(t + 1) * BLK]

    tail_row = lax.broadcasted_iota(jnp.int32, (EXT - nb, BLK), 0)
    ones3 = jnp.where(tail_row < 3, 1.0, 0.0)
    for hh in heads:
        qf = q_ref[0, pl.ds(pl.multiple_of(j * BLK, BLK), BLK), cols(hh)].astype(F32)
        qaug_ref[hh] = jnp.concatenate([(qf * (HEAD_DIM ** -0.5 * LOG2E)).T, mask_ref[hh, j], ones3],
                                       axis=0).astype(BF16)

    def scores(hh, gi):
        kg = kaug_ref[hh, pl.ds(pl.multiple_of(gi * (G * BLK), G * BLK), G * BLK), :]
        return jnp.dot(kg, qaug_ref[hh], preferred_element_type=F32)

    def update(hh, gi, z):
        m_old = m_ref[hh]
        m_new = jnp.maximum(m_old, jnp.max(z, axis=0, keepdims=True))
        alpha = jnp.exp2(m_old - m_new)
        p = jnp.exp2(z - m_new)
        m_ref[hh] = m_new
        l_ref[hh] = alpha * l_ref[hh] + jnp.sum(p, axis=0, keepdims=True)
        acc_ref[hh] = alpha * acc_ref[hh] + jnp.dot(vt_ref[hh, gi], p.astype(BF16),
                                                    preferred_element_type=F32)

    g_own = j // G
    last = jnp.maximum(g_own - 1, 0)
    in_grp = lax.broadcasted_iota(jnp.int32, (G, BLK, BLK), 0)
    own_mask = jnp.where(in_grp == j % G, causal_ref[...][None], 0.0)
    z_own = [scores(hh, g_own) for hh in heads]
    for hh in heads:
        za_ref[hh] = scores(hh, 0)
    for hh in heads:
        z = (z_own[hh].reshape(G, BLK, BLK) + own_mask).reshape(G * BLK, BLK)
        m0 = jnp.max(z, axis=0, keepdims=True)
        p = jnp.exp2(z - m0)
        m_ref[hh] = m0
        l_ref[hh] = jnp.sum(p, axis=0, keepdims=True)
        acc_ref[hh] = jnp.dot(vt_ref[hh, g_own], p.astype(BF16), preferred_element_type=F32)

    def body(t, _):
        g0 = 2 * t
        for hh in heads:
            zb_ref[hh] = scores(hh, jnp.minimum(g0 + 1, last))
        for hh in heads:
            update(hh, g0, za_ref[hh])

        @pl.when(g0 + 1 < g_own)
        def _():
            for hh in heads:
                za_ref[hh] = scores(hh, jnp.minimum(g0 + 2, last))
            for hh in heads:
                update(hh, g0 + 1, zb_ref[hh])
        return 0

    lax.fori_loop(0, (g_own + 1) // 2, body, 0)
    for hh in heads:
        o_ref[0, :, cols(hh)] = (acc_ref[hh] / l_ref[hh]).T.astype(o_ref.dtype)


def _moba(u3, slopes, q_col, k_col, v_col):
    B, S, _ = u3.shape
    BLK, dh, H, EXT = MOBA_BLOCK, HEAD_DIM, ATTN_HEADS, LANES
    nb = S // BLK
    G = min(MOBA_GROUP, nb)
    HB = MOBA_HEADS_PER_STEP
    assert nb % G == 0 and nb % SUBLANES == 0 and nb + 3 <= EXT
    assert H % HB == 0 and q_col % HB == 0 and k_col % HB == 0 and v_col % HB == 0
    return pl.pallas_call(
        functools.partial(_moba_kernel, G=G, HB=HB),
        grid=(B, H // HB, nb),
        in_specs=[
            pl.BlockSpec(memory_space=pltpu.SMEM),
            pl.BlockSpec((1, S, HB * dh), lambda b, h, j: (b, 0, q_col // HB + h)),
            pl.BlockSpec((1, S, HB * dh), lambda b, h, j: (b, 0, k_col // HB + h)),
            pl.BlockSpec((1, S, HB * dh), lambda b, h, j: (b, 0, v_col // HB + h)),
        ],
        out_specs=pl.BlockSpec((1, BLK, HB * dh), lambda b, h, j: (b, j, h)),
        out_shape=jax.ShapeDtypeStruct((B, S, H * dh), BF16),
        scratch_shapes=[
            pltpu.VMEM((HB, S, dh + EXT), BF16),
            pltpu.VMEM((HB, nb // G, dh, G * BLK), BF16),
            pltpu.VMEM((HB, nb, dh), F32),
            pltpu.VMEM((HB, nb, nb, BLK), F32),
            pltpu.VMEM((HB, dh + EXT, BLK), BF16),
            pltpu.VMEM((BLK, BLK), F32),
            pltpu.VMEM((HB, 1, BLK), F32),
            pltpu.VMEM((HB, 1, BLK), F32),
            pltpu.VMEM((HB, dh, BLK), F32),
            pltpu.VMEM((HB, G * BLK, BLK), F32),
            pltpu.VMEM((HB, G * BLK, BLK), F32),
        ],
        compiler_params=_cparams(("parallel", "parallel", "arbitrary")),
        name="moba",
    )(slopes, u3, u3, u3)


def _merge_kernel(yr_ref, ya_ref, gr0_ref, gr1_ref, ga0_ref, ga1_ref, x_ref,
                  pr_ref, pa_ref, wo_ref, g_ref, wr_ref, br_ref,
                  x1_ref, h2_ref, rt_ref):
    half = gr0_ref.shape[1]
    pr = jnp.dot(yr_ref[...], pr_ref[...], preferred_element_type=F32)
    pa = jnp.dot(ya_ref[...], pa_ref[...], preferred_element_type=F32)
    sig = lambda ref: jax.nn.sigmoid(ref[...].astype(F32))
    m0 = sig(gr0_ref) * pr[:, :half] + sig(ga0_ref) * pa[:, :half]
    m1 = sig(gr1_ref) * pr[:, half:] + sig(ga1_ref) * pa[:, half:]
    merged = jnp.concatenate([m0, m1], axis=-1).astype(BF16)
    x1 = x_ref[...] + jnp.dot(merged, wo_ref[...], preferred_element_type=F32)
    x1_ref[...] = x1
    ms = jnp.mean(x1 * x1, axis=-1, keepdims=True)
    h2 = x1 * lax.rsqrt(ms + RMS_EPS) * g_ref[...]
    _store_token_rows(h2_ref, h2)
    hi = h2.astype(BF16)
    lo = (h2 - hi.astype(F32)).astype(BF16)
    dot = functools.partial(jnp.dot, preferred_element_type=F32)
    rt_ref[...] = _route(dot(hi, wr_ref[0]) + dot(lo, wr_ref[0]) + dot(hi, wr_ref[1]) + br_ref[...])


def _hi_lo(w):
    hi = w.astype(BF16)
    return jnp.stack([hi, (w - hi.astype(F32)).astype(BF16)])


def _merge(y_rec, y_attn, u, gate_col, x2, pr, pa, wo, g, wr, br, tm=256):
    T, D = x2.shape
    W = y_rec.shape[1]
    half = D // 2
    tm = min(tm, T)
    gspec = lambda c: pl.BlockSpec((tm, half), lambda i: (i, gate_col + c))
    row = lambda n: pl.BlockSpec((tm, n), lambda i: (i, 0))
    return pl.pallas_call(
        _merge_kernel,
        grid=(T // tm,),
        in_specs=[
            row(W), row(W), gspec(0), gspec(1), gspec(2), gspec(3), row(D),
            _const_spec(pr.shape), _const_spec(pa.shape), _const_spec(wo.shape),
            _const_spec(g.shape), _const_spec(wr.shape), _const_spec(br.shape),
        ],
        out_specs=[row(D), pl.BlockSpec((tm * _token_rows(D), LANES), lambda i: (i, 0)), row(LANES)],
        out_shape=[
            jax.ShapeDtypeStruct((T, D), F32),
            jax.ShapeDtypeStruct((T * _token_rows(D), LANES), TOKEN_DTYPE),
            jax.ShapeDtypeStruct((T, LANES), F32),
        ],
        compiler_params=_cparams(("parallel",)),
        name="merge",
    )(y_rec, y_attn, u, u, u, u, x2, pr, pa, wo, g, wr, br)


def _route(lg):
    G, EPG = N_GROUPS, EXPERTS_PER_GROUP
    lane = lax.broadcasted_iota(jnp.int32, lg.shape, 1)

    def first_argmax(vals, mask):
        v = jnp.where(mask, vals, -jnp.inf)
        mx = jnp.max(v, axis=-1, keepdims=True)
        idx = jnp.min(jnp.where(jnp.logical_and(mask, v == mx), lane, LANES), axis=-1, keepdims=True)
        return mx, idx

    gmask = lane < G
    gmax, gi = first_argmax(lg, gmask)
    gexp = jnp.where(gmask, jnp.exp(lg - gmax), 0.0)
    grp_w = 1.0 / jnp.sum(gexp, axis=-1, keepdims=True)
    lo = G + gi * EPG
    emask = jnp.logical_and(lane >= lo, lane < lo + EPG)
    emax, i1 = first_argmax(lg, emask)
    eexp = jnp.where(emask, jnp.exp(lg - emax), 0.0)
    esum = jnp.sum(eexp, axis=-1, keepdims=True)
    p1 = 1.0 / esum
    mask2 = jnp.logical_and(emask, lane != i1)
    l2, i2 = first_argmax(lg, mask2)
    p2 = jnp.exp(l2 - emax) / esum
    den = p1 + p2
    w1 = grp_w * (p1 / den)
    w2 = grp_w * (p2 / den)
    e1 = (i1 - G).astype(F32)
    e2 = (i2 - G).astype(F32)
    out = jnp.where(lane == 0, e1, jnp.where(lane == 1, e2, jnp.where(lane == 2, w1, jnp.where(lane == 3, w2, 0.0))))
    return out


def _token_copy(src_hbm, src_tok, dst_ref, dst_tok, sem, c_n):
    return pltpu.make_async_copy(src_hbm.at[pl.ds(pl.multiple_of(src_tok * c_n, c_n), c_n), :],
                                 dst_ref.at[pl.ds(dst_tok * c_n, c_n), :], sem)


def _wait_tokens(src_hbm, dst_ref, sem):
    pltpu.make_async_copy(src_hbm.at[pl.ds(0, dst_ref.shape[0]), :], dst_ref, sem).wait()


def _ffn_kernel(te_ref, cur_ref, nxt_ref, h_hbm, wg_ref, wu_ref, wd_ref, o_ref,
                xbuf_ref, wgb_ref, wub_ref, wdb_ref, sem):
    i = pl.program_id(0)
    n = pl.num_programs(0)
    tr = cur_ref.shape[2]
    c_n = xbuf_ref.shape[1] // tr
    slot = i % 2
    parts = 4

    def gather(idx_ref, s, part):
        for r in range(part * tr // parts, (part + 1) * tr // parts):
            _token_copy(h_hbm, idx_ref[0, 0, r], xbuf_ref.at[s], r, sem.at[s], c_n).start(priority=r % 2)

    @pl.when(i == 0)
    def _():
        for part in range(parts):
            gather(cur_ref, 0, part)

    @pl.when(jnp.logical_or(i == 0, te_ref[i] != te_ref[jnp.maximum(i - 1, 0)]))
    def _():
        wgb_ref[...] = wg_ref[0].astype(BF16)
        wub_ref[...] = wu_ref[0].astype(BF16)
        wdb_ref[...] = wd_ref[0].astype(BF16)

    _wait_tokens(h_hbm, xbuf_ref.at[slot], sem.at[slot])
    x = _load_token_rows(xbuf_ref.at[slot], tr).astype(BF16)
    gather(nxt_ref, 1 - slot, 0)
    g = jnp.dot(x, wgb_ref[...], preferred_element_type=F32)
    gather(nxt_ref, 1 - slot, 1)
    u = jnp.dot(x, wub_ref[...], preferred_element_type=F32)
    hmid = (jax.nn.silu(g) * u).astype(BF16)
    gather(nxt_ref, 1 - slot, 2)
    y = jnp.dot(hmid, wdb_ref[...], preferred_element_type=F32)
    gather(nxt_ref, 1 - slot, 3)
    _store_token_rows(o_ref, y)

    @pl.when(i == n - 1)
    def _():
        _wait_tokens(h_hbm, xbuf_ref.at[1 - slot], sem.at[1 - slot])


def _expert_ffn(h2r, src_tok, tile_e, wg, wu, wd):
    D, Hd = wg.shape[1], wg.shape[2]
    c_n = _token_rows(D)
    n_tiles = tile_e.shape[0]
    idx = src_tok.reshape(n_tiles, 1, ROW_TILE)
    grid_spec = pltpu.PrefetchScalarGridSpec(
        num_scalar_prefetch=1,
        grid=(n_tiles,),
        in_specs=[
            pl.BlockSpec((1, 1, ROW_TILE), lambda i, te: (i, 0, 0), memory_space=pltpu.SMEM),
            pl.BlockSpec((1, 1, ROW_TILE), lambda i, te: (jnp.minimum(i + 1, n_tiles - 1), 0, 0),
                         memory_space=pltpu.SMEM),
            pl.BlockSpec(memory_space=pl.ANY),
            pl.BlockSpec((1, D, Hd), lambda i, te: (te[i], 0, 0)),
            pl.BlockSpec((1, D, Hd), lambda i, te: (te[i], 0, 0)),
            pl.BlockSpec((1, Hd, D), lambda i, te: (te[i], 0, 0)),
        ],
        out_specs=pl.BlockSpec((ROW_TILE * c_n, LANES), lambda i, te: (i, 0)),
        scratch_shapes=[
            pltpu.VMEM((2, ROW_TILE * c_n, LANES), TOKEN_DTYPE),
            pltpu.VMEM((D, Hd), BF16), pltpu.VMEM((D, Hd), BF16), pltpu.VMEM((Hd, D), BF16),
            pltpu.SemaphoreType.DMA((2,)),
        ],
    )
    return pl.pallas_call(
        _ffn_kernel,
        grid_spec=grid_spec,
        out_shape=jax.ShapeDtypeStruct((n_tiles * ROW_TILE * c_n, LANES), TOKEN_DTYPE),
        compiler_params=_cparams(("arbitrary",)),
        name="expert_ffn",
    )(tile_e, idx, idx, h2r, wg, wu, wd)


def _combine_kernel(cur_ref, nxt_ref, x1_ref, rt_ref, g_ref, y_hbm, o_ref, buf_ref, sem):
    i = pl.program_id(0)
    n = pl.num_programs(0)
    tm = x1_ref.shape[0]
    c_n = buf_ref.shape[1] // (2 * tm)
    slot = i % 2

    def gather(pos_ref, s):
        for r in range(tm):
            for k in range(2):
                _token_copy(y_hbm, pos_ref[0, 0, 2 * r + k], buf_ref.at[s], k * tm + r, sem.at[s], c_n).start()

    @pl.when(i == 0)
    def _():
        gather(cur_ref, 0)

    _wait_tokens(y_hbm, buf_ref.at[slot], sem.at[slot])
    gather(nxt_ref, 1 - slot)
    y = _load_token_rows(buf_ref.at[slot], 2 * tm)
    rt = rt_ref[...]
    x = x1_ref[...] + rt[:, 2:3] * y[:tm] + rt[:, 3:4] * y[tm:]
    ms = jnp.mean(x * x, axis=-1, keepdims=True)
    o_ref[...] = x * lax.rsqrt(ms + RMS_EPS) * g_ref[...]

    @pl.when(i == n - 1)
    def _():
        _wait_tokens(y_hbm, buf_ref.at[1 - slot], sem.at[1 - slot])


def _combine(x1, routed, pos, g, y, tm=128):
    T, D = x1.shape
    tm = min(tm, T)
    c_n = _token_rows(D)
    n = T // tm
    idx = pos.reshape(n, 1, 2 * tm)
    return pl.pallas_call(
        _combine_kernel,
        grid=(n,),
        in_specs=[
            pl.BlockSpec((1, 1, 2 * tm), lambda i: (i, 0, 0), memory_space=pltpu.SMEM),
            pl.BlockSpec((1, 1, 2 * tm), lambda i: (jnp.minimum(i + 1, n - 1), 0, 0),
                         memory_space=pltpu.SMEM),
            pl.BlockSpec((tm, D), lambda i: (i, 0)),
            pl.BlockSpec((tm, LANES), lambda i: (i, 0)),
            pl.BlockSpec((1, D), lambda i: (0, 0)),
            pl.BlockSpec(memory_space=pl.ANY),
        ],
        out_specs=pl.BlockSpec((tm, D), lambda i: (i, 0)),
        out_shape=jax.ShapeDtypeStruct((T, D), F32),
        scratch_shapes=[pltpu.VMEM((2, 2 * tm * c_n, LANES), TOKEN_DTYPE), pltpu.SemaphoreType.DMA((2,))],
        compiler_params=_cparams(("arbitrary",)),
        name="combine",
    )(idx, idx, x1, routed, g, y)


def _sorted_layout(eid, n_tiles):
    E, TR = N_EXPERTS, ROW_TILE
    eflat = eid.reshape(-1)
    A = eflat.shape[0]
    onehot = (eflat[:, None] == jnp.arange(E, dtype=jnp.int32)[None, :]).astype(jnp.int32)
    csum = jnp.cumsum(onehot, axis=0)
    counts = csum[-1]
    ptiles = (counts + TR - 1) // TR
    tile_end = jnp.cumsum(ptiles)
    pad_off = (tile_end - ptiles) * TR
    pos = jnp.sum(onehot * (csum - 1 + pad_off[None, :]), axis=1)
    assert n_tiles * TR == A + E * TR
    k = jnp.arange(TR, dtype=jnp.int32)[None, :]
    n_pad = (ptiles * TR - counts)[:, None]
    pad_key = jnp.where(k < n_pad, (pad_off + counts)[:, None] + k, n_tiles * TR)
    keys = jnp.concatenate([pos, pad_key.reshape(-1)])
    toks = jnp.concatenate([jnp.arange(A, dtype=jnp.int32) // 2, jnp.zeros((E * TR,), jnp.int32)])
    _, src_tok = lax.sort_key_val(keys, toks)
    tiles = jnp.arange(n_tiles, dtype=jnp.int32)
    tile_e_raw = jnp.sum((tile_end[None, :] <= tiles[:, None]).astype(jnp.int32), axis=1)
    tile_e = jnp.minimum(tile_e_raw, E - 1)
    return src_tok, pos.astype(jnp.int32), tile_e


def kernel(x, norm_attn_g, w_in, conv_w, conv_b, lru_wa, lru_ba, lru_wx, lru_bx, lru_lambda, proj_rec, proj_attn, w_out, norm_ffn_g, router_group_w, router_group_b, router_expert_w, router_expert_b, expert_w_gate, expert_w_up, expert_w_down, norm_final_g):
    B, S, D = x.shape
    T = B * S
    depth = w_in.shape[0]
    W = lru_lambda.shape[1]
    AW = ATTN_HEADS * HEAD_DIM
    assert W % HEAD_DIM == 0 and W == AW and S % MOBA_BLOCK == 0
    q_col = 2 * W // HEAD_DIM
    k_col = q_col + ATTN_HEADS
    v_col = k_col + ATTN_HEADS
    gate_col = (2 * W + 3 * AW) // (D // 2)
    assert gate_col * (D // 2) == 2 * W + 3 * AW
    slopes = 2.0 ** (-8.0 * jnp.arange(1, ATTN_HEADS + 1, dtype=F32) / ATTN_HEADS)
    assert (2 * T) % ROW_TILE == 0
    n_tiles = 2 * T // ROW_TILE + N_EXPERTS
    row = lambda v: v.reshape(1, -1)

    assert depth == 1, "the final norm is fused into the single layer's combine"
    l = 0
    x2 = x.reshape(T, D)
    u = _inproj(x2, row(norm_attn_g[l]), w_in[l].astype(BF16))
    u3 = u.reshape(B, S, -1)
    y_rec = _rglru(u3, conv_w[l], row(conv_b[l]), lru_wa[l].astype(BF16), row(lru_ba[l]),
                   lru_wx[l].astype(BF16), row(lru_bx[l]), row(lru_lambda[l]), W)
    y_attn = _moba(u3, slopes, q_col, k_col, v_col)
    wr = jnp.zeros((D, LANES), F32)
    wr = wr.at[:, :N_GROUPS].set(router_group_w[l]).at[:, N_GROUPS:N_GROUPS + N_EXPERTS].set(router_expert_w[l])
    br = jnp.zeros((1, LANES), F32)
    br = br.at[0, :N_GROUPS].set(router_group_b[l]).at[0, N_GROUPS:N_GROUPS + N_EXPERTS].set(router_expert_b[l])
    x1, h2, routed = _merge(y_rec.reshape(T, W), y_attn.reshape(T, AW), u, gate_col, x2,
                            proj_rec[l].astype(BF16), proj_attn[l].astype(BF16),
                            w_out[l].astype(BF16), row(norm_ffn_g[l]), _hi_lo(wr), br)
    eid = routed[:, :2].astype(jnp.int32)
    src_tok, pos, tile_e = _sorted_layout(eid, n_tiles)
    y = _expert_ffn(h2, src_tok, tile_e, expert_w_gate[l], expert_w_up[l], expert_w_down[l])
    out = _combine(x1, routed, pos, row(norm_final_g), y)
    return out.reshape(B, S, D)
```

```python
import functools

import jax
import jax.numpy as jnp
from jax import lax
from jax.experimental import pallas as pl
from jax.experimental.pallas import tpu as pltpu

F32 = jnp.float32
BF16 = jnp.bfloat16

RMS_EPS = 1e-6
LRU_BLOCKS = 8
CONV_WIDTH = 4
LRU_C = 8.0
ATTN_HEADS = 8
HEAD_DIM = 128
MOBA_BLOCK = 256
MOBA_TOPK = 3
N_GROUPS = 4
EXPERTS_PER_GROUP = 8
N_EXPERTS = N_GROUPS * EXPERTS_PER_GROUP

LANES = 128
SUBLANES = 8
NEG_BIG = -1e30
ROW_TILE = 256
VMEM_LIMIT = 56 * 1024 * 1024


def _cparams(sem):
    return pltpu.CompilerParams(dimension_semantics=sem, vmem_limit_bytes=VMEM_LIMIT)


def _const_spec(shape):
    nd = len(shape)
    return pl.BlockSpec(shape, lambda *_: (0,) * nd, pipeline_mode=pl.Buffered(1))


TOKEN_DTYPE = F32


def _token_rows(d):
    return d // LANES


def _store_token_rows(ref, val):
    n, d = val.shape
    c_n = _token_rows(d)
    for c in range(c_n):
        ref[pl.ds(c, n, stride=c_n), :] = val[:, c * LANES:(c + 1) * LANES]


def _load_token_rows(ref, n):
    c_n = ref.shape[0] // n
    return jnp.concatenate([ref[pl.ds(c, n, stride=c_n), :] for c in range(c_n)], axis=1)


def _inproj_kernel(x_ref, g_ref, w_ref, o_ref, h_ref, *, chunk):
    tm = x_ref.shape[0]

    @pl.when(pl.program_id(1) == 0)
    def _():
        def body(c, _):
            r0 = pl.multiple_of(c * chunk, chunk)
            x = x_ref[pl.ds(r0, chunk), :]
            ms = jnp.mean(x * x, axis=-1, keepdims=True)
            h_ref[pl.ds(r0, chunk), :] = (x * lax.rsqrt(ms + RMS_EPS) * g_ref[...]).astype(BF16)
            return 0
        lax.fori_loop(0, tm // chunk, body, 0)

    o_ref[...] = jnp.dot(h_ref[...], w_ref[...], preferred_element_type=F32).astype(o_ref.dtype)


def _inproj(x2, g, w_bf16, tm=1024, tn=1536):
    T, D = x2.shape
    N = w_bf16.shape[1]
    tm = min(tm, T)
    assert N % tn == 0
    return pl.pallas_call(
        functools.partial(_inproj_kernel, chunk=min(256, tm)),
        grid=(T // tm, N // tn),
        in_specs=[
            pl.BlockSpec((tm, D), lambda i, j: (i, 0)),
            pl.BlockSpec((1, D), lambda i, j: (0, 0)),
            pl.BlockSpec((D, tn), lambda i, j: (0, j)),
        ],
        out_specs=pl.BlockSpec((tm, tn), lambda i, j: (i, j)),
        out_shape=jax.ShapeDtypeStruct((T, N), BF16),
        scratch_shapes=[pltpu.VMEM((tm, D), BF16)],
        compiler_params=_cparams(("parallel", "arbitrary")),
        name="inproj",
    )(x2, g, w_bf16)


def _rglru_kernel(xr_ref, gr_ref, cw_ref, cb_ref, wa_ref, ba_ref, wx_ref, bx_ref, lam_ref,
                  o_ref, xbuf_ref, hc_ref):
    ts = xr_ref.shape[1]
    W = xr_ref.shape[2]
    bd = W // LRU_BLOCKS
    pad = SUBLANES

    @pl.when(pl.program_id(1) == 0)
    def _():
        xbuf_ref[0:pad, :] = jnp.zeros((pad, W), F32)
        hc_ref[...] = jnp.zeros_like(hc_ref)

    xbuf_ref[pad:pad + ts, :] = xr_ref[0].astype(F32)
    xc = cb_ref[...] + cw_ref[0:1, :] * xbuf_ref[pl.ds(pad - 3, ts), :]
    for k in range(1, CONV_WIDTH):
        xc = xc + cw_ref[k:k + 1, :] * xbuf_ref[pl.ds(pad - 3 + k, ts), :]
    xbuf_ref[0:pad, :] = xbuf_ref[ts:ts + pad, :]

    xcb = xc.astype(BF16)
    ra, ri = [], []
    for n in range(LRU_BLOCKS):
        blk = xcb[:, n * bd:(n + 1) * bd]
        ra.append(jnp.dot(blk, wa_ref[n], preferred_element_type=F32))
        ri.append(jnp.dot(blk, wx_ref[n], preferred_element_type=F32))
    r = jax.nn.sigmoid(jnp.concatenate(ra, axis=-1) + ba_ref[...])
    i = jax.nn.sigmoid(jnp.concatenate(ri, axis=-1) + bx_ref[...])
    lam = lam_ref[...]
    log_sig = jnp.minimum(lam, 0.0) - jnp.log1p(jnp.exp(-jnp.abs(lam)))
    log_a = LRU_C * r * log_sig
    a = jnp.exp(log_a)
    th = jnp.tanh(-log_a)
    b = jnp.sqrt(2.0 * th / (1.0 + th)) * (i * xc)

    row = lax.broadcasted_iota(jnp.int32, (ts, W), 0)
    d = 1
    while d < ts:
        if d % SUBLANES:
            keep = row >= d
            a_sh = jnp.where(keep, pltpu.roll(a, d, 0), 1.0)
            b_sh = jnp.where(keep, pltpu.roll(b, d, 0), 0.0)
        else:
            a_sh = jnp.concatenate([jnp.ones((d, W), F32), a[:ts - d]], axis=0)
            b_sh = jnp.concatenate([jnp.zeros((d, W), F32), b[:ts - d]], axis=0)
        b = a * b_sh + b
        a = a * a_sh
        d *= 2
    h = b + a * hc_ref[...]
    hc_ref[...] = h[ts - 1:ts, :]

    o_ref[0] = (h * jax.nn.gelu(gr_ref[0].astype(F32), approximate=True)).astype(o_ref.dtype)


def _rglru(u3, conv_w, conv_b, wa, ba, wx, bx, lam, W, ts=256):
    B, S, _ = u3.shape
    ts = min(ts, S)
    vec = lambda: pl.BlockSpec((1, W), lambda b, s: (0, 0))
    bd = W // LRU_BLOCKS
    return pl.pallas_call(
        _rglru_kernel,
        grid=(B, S // ts),
        in_specs=[
            pl.BlockSpec((1, ts, W), lambda b, s: (b, s, 0)),
            pl.BlockSpec((1, ts, W), lambda b, s: (b, s, 1)),
            pl.BlockSpec((CONV_WIDTH, W), lambda b, s: (0, 0)),
            vec(),
            pl.BlockSpec((LRU_BLOCKS, bd, bd), lambda b, s: (0, 0, 0)),
            vec(),
            pl.BlockSpec((LRU_BLOCKS, bd, bd), lambda b, s: (0, 0, 0)),
            vec(),
            vec(),
        ],
        out_specs=pl.BlockSpec((1, ts, W), lambda b, s: (b, s, 0)),
        out_shape=jax.ShapeDtypeStruct((B, S, W), BF16),
        scratch_shapes=[pltpu.VMEM((SUBLANES + ts, W), F32), pltpu.VMEM((1, W), F32)],
        compiler_params=_cparams(("parallel", "arbitrary")),
        name="rglru",
    )(u3, u3, conv_w, conv_b, wa, ba, wx, bx, lam)


LOG2E = 1.4426950408889634
MOBA_GROUP = 4
MOBA_HEADS_PER_STEP = 2
MOBA_GATE_CHUNK = 8


def _split3(c):
    c0 = c.astype(BF16).astype(F32)
    r1 = c - c0
    c1 = r1.astype(BF16).astype(F32)
    c2 = (r1 - c1).astype(BF16).astype(F32)
    return c0, c1, c2


def _moba_kernel(slope_ref, q_ref, k_ref, v_ref, o_ref,
                 kaug_ref, vt_ref, kmean_ref, mask_ref, qaug_ref, causal_ref, m_ref, l_ref, acc_ref,
                 za_ref, zb_ref, *, G, HB):
    BLK, dh, EXT = MOBA_BLOCK, HEAD_DIM, LANES
    nb = k_ref.shape[1] // BLK
    j = pl.program_id(2)
    heads = range(HB)
    cols = lambda hh: slice(hh * dh, (hh + 1) * dh)

    @pl.when(j == 0)
    def _():
        lane = lax.broadcasted_iota(jnp.int32, (BLK, EXT), 1)
        rowi = lax.broadcasted_iota(jnp.int32, (BLK, EXT), 0)
        for hh in heads:
            slope = slope_ref[pl.program_id(1) * HB + hh]
            for n in range(nb):
                rows = slice(n * BLK, (n + 1) * BLK)
                kb = k_ref[0, rows, cols(hh)]
                kmean_ref[hh, n:n + 1, :] = jnp.mean(kb.astype(F32), axis=0, keepdims=True)
                c0, c1, c2 = _split3((LOG2E * slope) * (rowi + n * BLK).astype(F32))
                ext = jnp.where(lane == n, 1.0,
                                jnp.where(lane == nb, c0, jnp.where(lane == nb + 1, c1,
                                                                    jnp.where(lane == nb + 2, c2, 0.0))))
                kaug_ref[hh, rows, 0:dh] = kb
                kaug_ref[hh, rows, dh:dh + EXT] = ext.astype(BF16)
                vt_ref[hh, n // G, :, (n % G) * BLK:(n % G + 1) * BLK] = (
                    v_ref[0, rows, cols(hh)].astype(F32).T.astype(BF16))
        kk = lax.broadcasted_iota(jnp.int32, (BLK, BLK), 0)
        qq = lax.broadcasted_iota(jnp.int32, (BLK, BLK), 1)
        causal_ref[...] = jnp.where(kk <= qq, 0.0, NEG_BIG)

        qc = min(MOBA_GATE_CHUNK, nb) * BLK
        blk_i = lax.broadcasted_iota(jnp.int32, (nb, qc), 0)
        lane_i = lax.broadcasted_iota(jnp.int32, (nb, qc), 1)
        for hh in heads:
            for c0 in range(0, nb * BLK, qc):
                qa = q_ref[0, c0:c0 + qc, cols(hh)].astype(F32)
                gate = lax.dot_general(kmean_ref[hh], qa, (((1,), (1,)), ((), ())),
                                       preferred_element_type=F32,
                                       precision=lax.Precision.HIGHEST)
                qpos = lane_i + c0
                past = (blk_i + 1) * BLK <= qpos
                g = jnp.where(past, gate, -jnp.inf)
                sel = jnp.logical_and(blk_i * BLK <= qpos, jnp.logical_not(past))
                for _ in range(min(MOBA_TOPK, nb)):
                    cmax = jnp.max(g, axis=0, keepdims=True)
                    first = jnp.min(jnp.where(g == cmax, blk_i, nb), axis=0, keepdims=True)
                    hit = blk_i == first
                    sel = jnp.logical_or(sel, jnp.logical_and(hit, past))
                    g = jnp.where(hit, -jnp.inf, g)
                mask = jnp.where(sel, 0.0, NEG_BIG)
                for t in range(qc // BLK):
                    mask_ref[hh, c0 // BLK + t] = mask[:, t * BLK:(t + 1) * BLK]

    tail_row = lax.broadcasted_iota(jnp.int32, (EXT - nb, BLK), 0)
    ones3 = jnp.where(tail_row < 3, 1.0, 0.0)
    for hh in heads:
        qf = q_ref[0, pl.ds(pl.multiple_of(j * BLK, BLK), BLK), cols(hh)].astype(F32)
        qaug_ref[hh] = jnp.concatenate([(qf * (HEAD_DIM ** -0.5 * LOG2E)).T, mask_ref[hh, j], ones3],
                                       axis=0).astype(BF16)

    def scores(hh, gi):
        kg = kaug_ref[hh, pl.ds(pl.multiple_of(gi * (G * BLK), G * BLK), G * BLK), :]
        return jnp.dot(kg, qaug_ref[hh], preferred_element_type=F32)

    def update(hh, gi, z):
        m_old = m_ref[hh]
        m_new = jnp.maximum(m_old, jnp.max(z, axis=0, keepdims=True))
        alpha = jnp.exp2(m_old - m_new)
        p = jnp.exp2(z - m_new)
        m_ref[hh] = m_new
        l_ref[hh] = alpha * l_ref[hh] + jnp.sum(p, axis=0, keepdims=True)
        acc_ref[hh] = alpha * acc_ref[hh] + jnp.dot(vt_ref[hh, gi], p.astype(BF16),
                                                    preferred_element_type=F32)

    g_own = j // G
    last = jnp.maximum(g_own - 1, 0)
    in_grp = lax.broadcasted_iota(jnp.int32, (G, BLK, BLK), 0)
    own_mask = jnp.where(in_grp == j % G, causal_ref[...][None], 0.0)
    z_own = [scores(hh, g_own) for hh in heads]
    for hh in heads:
        za_ref[hh] = scores(hh, 0)
    for hh in heads:
        z = (z_own[hh].reshape(G, BLK, BLK) + own_mask).reshape(G * BLK, BLK)
        m0 = jnp.max(z, axis=0, keepdims=True)
        p = jnp.exp2(z - m0)
        m_ref[hh] = m0
        l_ref[hh] = jnp.sum(p, axis=0, keepdims=True)
        acc_ref[hh] = jnp.dot(vt_ref[hh, g_own], p.astype(BF16), preferred_element_type=F32)

    def body(t, _):
        g0 = 2 * t
        for hh in heads:
            zb_ref[hh] = scores(hh, jnp.minimum(g0 + 1, last))
        for hh in heads:
            update(hh, g0, za_ref[hh])

        @pl.when(g0 + 1 < g_own)
        def _():
            for hh in heads:
                za_ref[hh] = scores(hh, jnp.minimum(g0 + 2, last))
            for hh in heads:
                update(hh, g0 + 1, zb_ref[hh])
        return 0

    lax.fori_loop(0, (g_own + 1) // 2, body, 0)
    for hh in heads:
        o_ref[0, :, cols(hh)] = (acc_ref[hh] / l_ref[hh]).T.astype(o_ref.dtype)


def _moba(u3, slopes, q_col, k_col, v_col):
    B, S, _ = u3.shape
    BLK, dh, H, EXT = MOBA_BLOCK, HEAD_DIM, ATTN_HEADS, LANES
    nb = S // BLK
    G = min(MOBA_GROUP, nb)
    HB = MOBA_HEADS_PER_STEP
    assert nb % G == 0 and nb % SUBLANES == 0 and nb + 3 <= EXT
    assert H % HB == 0 and q_col % HB == 0 and k_col % HB == 0 and v_col % HB == 0
    return pl.pallas_call(
        functools.partial(_moba_kernel, G=G, HB=HB),
        grid=(B, H // HB, nb),
        in_specs=[
            pl.BlockSpec(memory_space=pltpu.SMEM),
            pl.BlockSpec((1, S, HB * dh), lambda b, h, j: (b, 0, q_col // HB + h)),
            pl.BlockSpec((1, S, HB * dh), lambda b, h, j: (b, 0, k_col // HB + h)),
            pl.BlockSpec((1, S, HB * dh), lambda b, h, j: (b, 0, v_col // HB + h)),
        ],
        out_specs=pl.BlockSpec((1, BLK, HB * dh), lambda b, h, j: (b, j, h)),
        out_shape=jax.ShapeDtypeStruct((B, S, H * dh), BF16),
        scratch_shapes=[
            pltpu.VMEM((HB, S, dh + EXT), BF16),
            pltpu.VMEM((HB, nb // G, dh, G * BLK), BF16),
            pltpu.VMEM((HB, nb, dh), F32),
            pltpu.VMEM((HB, nb, nb, BLK), F32),
            pltpu.VMEM((HB, dh + EXT, BLK), BF16),
            pltpu.VMEM((BLK, BLK), F32),
            pltpu.VMEM((HB, 1, BLK), F32),
            pltpu.VMEM((HB, 1, BLK), F32),
            pltpu.VMEM((HB, dh, BLK), F32),
            pltpu.VMEM((HB, G * BLK, BLK), F32),
            pltpu.VMEM((HB, G * BLK, BLK), F32),
        ],
        compiler_params=_cparams(("parallel", "parallel", "arbitrary")),
        name="moba",
    )(slopes, u3, u3, u3)


def _merge_kernel(yr_ref, ya_ref, gr0_ref, gr1_ref, ga0_ref, ga1_ref, x_ref,
                  pr_ref, pa_ref, wo_ref, g_ref, wr_ref, br_ref,
                  x1_ref, h2_ref, rt_ref):
    half = gr0_ref.shape[1]
    pr = jnp.dot(yr_ref[...], pr_ref[...], preferred_element_type=F32)
    pa = jnp.dot(ya_ref[...], pa_ref[...], preferred_element_type=F32)
    sig = lambda ref: jax.nn.sigmoid(ref[...].astype(F32))
    m0 = sig(gr0_ref) * pr[:, :half] + sig(ga0_ref) * pa[:, :half]
    m1 = sig(gr1_ref) * pr[:, half:] + sig(ga1_ref) * pa[:, half:]
    merged = jnp.concatenate([m0, m1], axis=-1).astype(BF16)
    x1 = x_ref[...] + jnp.dot(merged, wo_ref[...], preferred_element_type=F32)
    x1_ref[...] = x1
    ms = jnp.mean(x1 * x1, axis=-1, keepdims=True)
    h2 = x1 * lax.rsqrt(ms + RMS_EPS) * g_ref[...]
    _store_token_rows(h2_ref, h2)
    hi = h2.astype(BF16)
    lo = (h2 - hi.astype(F32)).astype(BF16)
    dot = functools.partial(jnp.dot, preferred_element_type=F32)
    rt_ref[...] = _route(dot(hi, wr_ref[0]) + dot(lo, wr_ref[0]) + dot(hi, wr_ref[1]) + br_ref[...])


def _hi_lo(w):
    hi = w.astype(BF16)
    return jnp.stack([hi, (w - hi.astype(F32)).astype(BF16)])


def _merge(y_rec, y_attn, u, gate_col, x2, pr, pa, wo, g, wr, br, tm=256):
    T, D = x2.shape
    W = y_rec.shape[1]
    half = D // 2
    tm = min(tm, T)
    gspec = lambda c: pl.BlockSpec((tm, half), lambda i: (i, gate_col + c))
    row = lambda n: pl.BlockSpec((tm, n), lambda i: (i, 0))
    return pl.pallas_call(
        _merge_kernel,
        grid=(T // tm,),
        in_specs=[
            row(W), row(W), gspec(0), gspec(1), gspec(2), gspec(3), row(D),
            _const_spec(pr.shape), _const_spec(pa.shape), _const_spec(wo.shape),
            _const_spec(g.shape), _const_spec(wr.shape), _const_spec(br.shape),
        ],
        out_specs=[row(D), pl.BlockSpec((tm * _token_rows(D), LANES), lambda i: (i, 0)), row(LANES)],
        out_shape=[
            jax.ShapeDtypeStruct((T, D), F32),
            jax.ShapeDtypeStruct((T * _token_rows(D), LANES), TOKEN_DTYPE),
            jax.ShapeDtypeStruct((T, LANES), F32),
        ],
        compiler_params=_cparams(("parallel",)),
        name="merge",
    )(y_rec, y_attn, u, u, u, u, x2, pr, pa, wo, g, wr, br)


def _route(lg):
    G, EPG = N_GROUPS, EXPERTS_PER_GROUP
    lane = lax.broadcasted_iota(jnp.int32, lg.shape, 1)

    def first_argmax(vals, mask):
        v = jnp.where(mask, vals, -jnp.inf)
        mx = jnp.max(v, axis=-1, keepdims=True)
        idx = jnp.min(jnp.where(jnp.logical_and(mask, v == mx), lane, LANES), axis=-1, keepdims=True)
        return mx, idx

    gmask = lane < G
    gmax, gi = first_argmax(lg, gmask)
    gexp = jnp.where(gmask, jnp.exp(lg - gmax), 0.0)
    grp_w = 1.0 / jnp.sum(gexp, axis=-1, keepdims=True)
    lo = G + gi * EPG
    emask = jnp.logical_and(lane >= lo, lane < lo + EPG)
    emax, i1 = first_argmax(lg, emask)
    eexp = jnp.where(emask, jnp.exp(lg - emax), 0.0)
    esum = jnp.sum(eexp, axis=-1, keepdims=True)
    p1 = 1.0 / esum
    mask2 = jnp.logical_and(emask, lane != i1)
    l2, i2 = first_argmax(lg, mask2)
    p2 = jnp.exp(l2 - emax) / esum
    den = p1 + p2
    w1 = grp_w * (p1 / den)
    w2 = grp_w * (p2 / den)
    e1 = (i1 - G).astype(F32)
    e2 = (i2 - G).astype(F32)
    out = jnp.where(lane == 0, e1, jnp.where(lane == 1, e2, jnp.where(lane == 2, w1, jnp.where(lane == 3, w2, 0.0))))
    return out


def _token_copy(src_hbm, src_tok, dst_ref, dst_tok, sem, c_n):
    return pltpu.make_async_copy(src_hbm.at[pl.ds(pl.multiple_of(src_tok * c_n, c_n), c_n), :],
                                 dst_ref.at[pl.ds(dst_tok * c_n, c_n), :], sem)


def _wait_tokens(src_hbm, dst_ref, sem):
    pltpu.make_async_copy(src_hbm.at[pl.ds(0, dst_ref.shape[0]), :], dst_ref, sem).wait()


def _ffn_kernel(te_ref, cur_ref, nxt_ref, h_hbm, wg_ref, wu_ref, wd_ref, o_ref,
                xbuf_ref, wgb_ref, wub_ref, wdb_ref, sem):
    i = pl.program_id(0)
    n = pl.num_programs(0)
    tr = cur_ref.shape[2]
    c_n = xbuf_ref.shape[1] // tr
    slot = i % 2
    parts = 4

    def gather(idx_ref, s, part):
        for r in range(part * tr // parts, (part + 1) * tr // parts):
            _token_copy(h_hbm, idx_ref[0, 0, r], xbuf_ref.at[s], r, sem.at[s], c_n).start(priority=r % 2)

    @pl.when(i == 0)
    def _():
        for part in range(parts):
            gather(cur_ref, 0, part)

    @pl.when(jnp.logical_or(i == 0, te_ref[i] != te_ref[jnp.maximum(i - 1, 0)]))
    def _():
        wgb_ref[...] = wg_ref[0].astype(BF16)
        wub_ref[...] = wu_ref[0].astype(BF16)
        wdb_ref[...] = wd_ref[0].astype(BF16)

    _wait_tokens(h_hbm, xbuf_ref.at[slot], sem.at[slot])
    x = _load_token_rows(xbuf_ref.at[slot], tr).astype(BF16)
    gather(nxt_ref, 1 - slot, 0)
    g = jnp.dot(x, wgb_ref[...], preferred_element_type=F32)
    gather(nxt_ref, 1 - slot, 1)
    u = jnp.dot(x, wub_ref[...], preferred_element_type=F32)
    hmid = (jax.nn.silu(g) * u).astype(BF16)
    gather(nxt_ref, 1 - slot, 2)
    y = jnp.dot(hmid, wdb_ref[...], preferred_element_type=F32)
    gather(nxt_ref, 1 - slot, 3)
    _store_token_rows(o_ref, y)

    @pl.when(i == n - 1)
    def _():
        _wait_tokens(h_hbm, xbuf_ref.at[1 - slot], sem.at[1 - slot])


def _expert_ffn(h2r, src_tok, tile_e, wg, wu, wd):
    D, Hd = wg.shape[1], wg.shape[2]
    c_n = _token_rows(D)
    n_tiles = tile_e.shape[0]
    idx = src_tok.reshape(n_tiles, 1, ROW_TILE)
    grid_spec = pltpu.PrefetchScalarGridSpec(
        num_scalar_prefetch=1,
        grid=(n_tiles,),
        in_specs=[
            pl.BlockSpec((1, 1, ROW_TILE), lambda i, te: (i, 0, 0), memory_space=pltpu.SMEM),
            pl.BlockSpec((1, 1, ROW_TILE), lambda i, te: (jnp.minimum(i + 1, n_tiles - 1), 0, 0),
                         memory_space=pltpu.SMEM),
            pl.BlockSpec(memory_space=pl.ANY),
            pl.BlockSpec((1, D, Hd), lambda i, te: (te[i], 0, 0)),
            pl.BlockSpec((1, D, Hd), lambda i, te: (te[i], 0, 0)),
            pl.BlockSpec((1, Hd, D), lambda i, te: (te[i], 0, 0)),
        ],
        out_specs=pl.BlockSpec((ROW_TILE * c_n, LANES), lambda i, te: (i, 0)),
        scratch_shapes=[
            pltpu.VMEM((2, ROW_TILE * c_n, LANES), TOKEN_DTYPE),
            pltpu.VMEM((D, Hd), BF16), pltpu.VMEM((D, Hd), BF16), pltpu.VMEM((Hd, D), BF16),
            pltpu.SemaphoreType.DMA((2,)),
        ],
    )
    return pl.pallas_call(
        _ffn_kernel,
        grid_spec=grid_spec,
        out_shape=jax.ShapeDtypeStruct((n_tiles * ROW_TILE * c_n, LANES), TOKEN_DTYPE),
        compiler_params=_cparams(("arbitrary",)),
        name="expert_ffn",
    )(tile_e, idx, idx, h2r, wg, wu, wd)


def _combine_kernel(cur_ref, nxt_ref, x1_ref, rt_ref, g_ref, y_hbm, o_ref, buf_ref, sem):
    i = pl.program_id(0)
    n = pl.num_programs(0)
    tm = x1_ref.shape[0]
    c_n = buf_ref.shape[1] // (2 * tm)
    slot = i % 2

    def gather(pos_ref, s):
        for r in range(tm):
            for k in range(2):
                _token_copy(y_hbm, pos_ref[0, 0, 2 * r + k], buf_ref.at[s], k * tm + r, sem.at[s], c_n).start()

    @pl.when(i == 0)
    def _():
        gather(cur_ref, 0)

    _wait_tokens(y_hbm, buf_ref.at[slot], sem.at[slot])
    gather(nxt_ref, 1 - slot)
    y = _load_token_rows(buf_ref.at[slot], 2 * tm)
    rt = rt_ref[...]
    x = x1_ref[...] + rt[:, 2:3] * y[:tm] + rt[:, 3:4] * y[tm:]
    ms = jnp.mean(x * x, axis=-1, keepdims=True)
    o_ref[...] = x * lax.rsqrt(ms + RMS_EPS) * g_ref[...]

    @pl.when(i == n - 1)
    def _():
        _wait_tokens(y_hbm, buf_ref.at[1 - slot], sem.at[1 - slot])


def _combine(x1, routed, pos, g, y, tm=256):
    T, D = x1.shape
    tm = min(tm, T)
    c_n = _token_rows(D)
    n = T // tm
    idx = pos.reshape(n, 1, 2 * tm)
    return pl.pallas_call(
        _combine_kernel,
        grid=(n,),
        in_specs=[
            pl.BlockSpec((1, 1, 2 * tm), lambda i: (i, 0, 0), memory_space=pltpu.SMEM),
            pl.BlockSpec((1, 1, 2 * tm), lambda i: (jnp.minimum(i + 1, n - 1), 0, 0),
                         memory_space=pltpu.SMEM),
            pl.BlockSpec((tm, D), lambda i: (i, 0)),
            pl.BlockSpec((tm, LANES), lambda i: (i, 0)),
            pl.BlockSpec((1, D), lambda i: (0, 0)),
            pl.BlockSpec(memory_space=pl.ANY),
        ],
        out_specs=pl.BlockSpec((tm, D), lambda i: (i, 0)),
        out_shape=jax.ShapeDtypeStruct((T, D), F32),
        scratch_shapes=[pltpu.VMEM((2, 2 * tm * c_n, LANES), TOKEN_DTYPE), pltpu.SemaphoreType.DMA((2,))],
        compiler_params=_cparams(("arbitrary",)),
        name="combine",
    )(idx, idx, x1, routed, g, y)


def _sorted_layout(eid, n_tiles):
    E, TR = N_EXPERTS, ROW_TILE
    eflat = eid.reshape(-1)
    A = eflat.shape[0]
    onehot = (eflat[:, None] == jnp.arange(E, dtype=jnp.int32)[None, :]).astype(jnp.int32)
    csum = jnp.cumsum(onehot, axis=0)
    counts = csum[-1]
    ptiles = (counts + TR - 1) // TR
    tile_end = jnp.cumsum(ptiles)
    pad_off = (tile_end - ptiles) * TR
    pos = jnp.sum(onehot * (csum - 1 + pad_off[None, :]), axis=1)
    assert n_tiles * TR == A + E * TR
    k = jnp.arange(TR, dtype=jnp.int32)[None, :]
    n_pad = (ptiles * TR - counts)[:, None]
    pad_key = jnp.where(k < n_pad, (pad_off + counts)[:, None] + k, n_tiles * TR)
    keys = jnp.concatenate([pos, pad_key.reshape(-1)])
    toks = jnp.concatenate([jnp.arange(A, dtype=jnp.int32) // 2, jnp.zeros((E * TR,), jnp.int32)])
    _, src_tok = lax.sort_key_val(keys, toks)
    tiles = jnp.arange(n_tiles, dtype=jnp.int32)
    tile_e_raw = jnp.sum((tile_end[None, :] <= tiles[:, None]).astype(jnp.int32), axis=1)
    tile_e = jnp.minimum(tile_e_raw, E - 1)
    return src_tok, pos.astype(jnp.int32), tile_e


def kernel(x, norm_attn_g, w_in, conv_w, conv_b, lru_wa, lru_ba, lru_wx, lru_bx, lru_lambda, proj_rec, proj_attn, w_out, norm_ffn_g, router_group_w, router_group_b, router_expert_w, router_expert_b, expert_w_gate, expert_w_up, expert_w_down, norm_final_g):
    B, S, D = x.shape
    T = B * S
    depth = w_in.shape[0]
    W = lru_lambda.shape[1]
    AW = ATTN_HEADS * HEAD_DIM
    assert W % HEAD_DIM == 0 and W == AW and S % MOBA_BLOCK == 0
    q_col = 2 * W // HEAD_DIM
    k_col = q_col + ATTN_HEADS
    v_col = k_col + ATTN_HEADS
    gate_col = (2 * W + 3 * AW) // (D // 2)
    assert gate_col * (D // 2) == 2 * W + 3 * AW
    slopes = 2.0 ** (-8.0 * jnp.arange(1, ATTN_HEADS + 1, dtype=F32) / ATTN_HEADS)
    assert (2 * T) % ROW_TILE == 0
    n_tiles = 2 * T // ROW_TILE + N_EXPERTS
    row = lambda v: v.reshape(1, -1)

    assert depth == 1, "the final norm is fused into the single layer's combine"
    l = 0
    x2 = x.reshape(T, D)
    u = _inproj(x2, row(norm_attn_g[l]), w_in[l].astype(BF16))
    u3 = u.reshape(B, S, -1)
    y_rec = _rglru(u3, conv_w[l], row(conv_b[l]), lru_wa[l].astype(BF16), row(lru_ba[l]),
                   lru_wx[l].astype(BF16), row(lru_bx[l]), row(lru_lambda[l]), W)
    y_attn = _moba(u3, slopes, q_col, k_col, v_col)
    wr = jnp.zeros((D, LANES), F32)
    wr = wr.at[:, :N_GROUPS].set(router_group_w[l]).at[:, N_GROUPS:N_GROUPS + N_EXPERTS].set(router_expert_w[l])
    br = jnp.zeros((1, LANES), F32)
    br = br.at[0, :N_GROUPS].set(router_group_b[l]).at[0, N_GROUPS:N_GROUPS + N_EXPERTS].set(router_expert_b[l])
    x1, h2, routed = _merge(y_rec.reshape(T, W), y_attn.reshape(T, AW), u, gate_col, x2,
                            proj_rec[l].astype(BF16), proj_attn[l].astype(BF16),
                            w_out[l].astype(BF16), row(norm_ffn_g[l]), _hi_lo(wr), br)
    eid = routed[:, :2].astype(jnp.int32)
    src_tok, pos, tile_e = _sorted_layout(eid, n_tiles)
    y = _expert_ffn(h2, src_tok, tile_e, expert_w_gate[l], expert_w_up[l], expert_w_down[l])
    out = _combine(x1, routed, pos, row(norm_final_g), y)
    return out.reshape(B, S, D)
```

```python
import functools

import jax
import jax.numpy as jnp
from jax import lax
from jax.experimental import pallas as pl
from jax.experimental.pallas import tpu as pltpu

F32 = jnp.float32
BF16 = jnp.bfloat16

RMS_EPS = 1e-6
LRU_BLOCKS = 8
CONV_WIDTH = 4
LRU_C = 8.0
ATTN_HEADS = 8
HEAD_DIM = 128
MOBA_BLOCK = 256
MOBA_TOPK = 3
N_GROUPS = 4
EXPERTS_PER_GROUP = 8
N_EXPERTS = N_GROUPS * EXPERTS_PER_GROUP

LANES = 128
SUBLANES = 8
NEG_BIG = -1e30
ROW_TILE = 256
VMEM_LIMIT = 56 * 1024 * 1024


def _cparams(sem):
    return pltpu.CompilerParams(dimension_semantics=sem, vmem_limit_bytes=VMEM_LIMIT)


def _const_spec(shape):
    nd = len(shape)
    return pl.BlockSpec(shape, lambda *_: (0,) * nd, pipeline_mode=pl.Buffered(1))


TOKEN_DTYPE = F32


def _token_rows(d):
    return d // LANES


def _store_token_rows(ref, val):
    n, d = val.shape
    c_n = _token_rows(d)
    for c in range(c_n):
        ref[pl.ds(c, n, stride=c_n), :] = val[:, c * LANES:(c + 1) * LANES]


def _load_token_rows(ref, n):
    c_n = ref.shape[0] // n
    return jnp.concatenate([ref[pl.ds(c, n, stride=c_n), :] for c in range(c_n)], axis=1)


def _inproj_kernel(x_ref, g_ref, w_ref, o_ref, h_ref, *, chunk):
    tm = x_ref.shape[0]

    @pl.when(pl.program_id(1) == 0)
    def _():
        def body(c, _):
            r0 = pl.multiple_of(c * chunk, chunk)
            x = x_ref[pl.ds(r0, chunk), :]
            ms = jnp.mean(x * x, axis=-1, keepdims=True)
            h_ref[pl.ds(r0, chunk), :] = (x * lax.rsqrt(ms + RMS_EPS) * g_ref[...]).astype(BF16)
            return 0
        lax.fori_loop(0, tm // chunk, body, 0)

    o_ref[...] = jnp.dot(h_ref[...], w_ref[...], preferred_element_type=F32).astype(o_ref.dtype)


def _inproj(x2, g, w_bf16, tm=1024, tn=2304):
    T, D = x2.shape
    N = w_bf16.shape[1]
    tm = min(tm, T)
    assert N % tn == 0
    return pl.pallas_call(
        functools.partial(_inproj_kernel, chunk=min(256, tm)),
        grid=(T // tm, N // tn),
        in_specs=[
            pl.BlockSpec((tm, D), lambda i, j: (i, 0)),
            pl.BlockSpec((1, D), lambda i, j: (0, 0)),
            pl.BlockSpec((D, tn), lambda i, j: (0, j)),
        ],
        out_specs=pl.BlockSpec((tm, tn), lambda i, j: (i, j)),
        out_shape=jax.ShapeDtypeStruct((T, N), BF16),
        scratch_shapes=[pltpu.VMEM((tm, D), BF16)],
        compiler_params=_cparams(("parallel", "arbitrary")),
        name="inproj",
    )(x2, g, w_bf16)


def _rglru_kernel(xr_ref, gr_ref, cw_ref, cb_ref, wa_ref, ba_ref, wx_ref, bx_ref, lam_ref,
                  o_ref, xbuf_ref, hc_ref):
    ts = xr_ref.shape[1]
    W = xr_ref.shape[2]
    bd = W // LRU_BLOCKS
    pad = SUBLANES

    @pl.when(pl.program_id(1) == 0)
    def _():
        xbuf_ref[0:pad, :] = jnp.zeros((pad, W), F32)
        hc_ref[...] = jnp.zeros_like(hc_ref)

    xbuf_ref[pad:pad + ts, :] = xr_ref[0].astype(F32)
    xc = cb_ref[...] + cw_ref[0:1, :] * xbuf_ref[pl.ds(pad - 3, ts), :]
    for k in range(1, CONV_WIDTH):
        xc = xc + cw_ref[k:k + 1, :] * xbuf_ref[pl.ds(pad - 3 + k, ts), :]
    xbuf_ref[0:pad, :] = xbuf_ref[ts:ts + pad, :]

    xcb = xc.astype(BF16)
    ra, ri = [], []
    for n in range(LRU_BLOCKS):
        blk = xcb[:, n * bd:(n + 1) * bd]
        ra.append(jnp.dot(blk, wa_ref[n], preferred_element_type=F32))
        ri.append(jnp.dot(blk, wx_ref[n], preferred_element_type=F32))
    r = jax.nn.sigmoid(jnp.concatenate(ra, axis=-1) + ba_ref[...])
    i = jax.nn.sigmoid(jnp.concatenate(ri, axis=-1) + bx_ref[...])
    lam = lam_ref[...]
    log_sig = jnp.minimum(lam, 0.0) - jnp.log1p(jnp.exp(-jnp.abs(lam)))
    log_a = LRU_C * r * log_sig
    a = jnp.exp(log_a)
    th = jnp.tanh(-log_a)
    b = jnp.sqrt(2.0 * th / (1.0 + th)) * (i * xc)

    row = lax.broadcasted_iota(jnp.int32, (ts, W), 0)
    d = 1
    while d < ts:
        if d % SUBLANES:
            keep = row >= d
            a_sh = jnp.where(keep, pltpu.roll(a, d, 0), 1.0)
            b_sh = jnp.where(keep, pltpu.roll(b, d, 0), 0.0)
        else:
            a_sh = jnp.concatenate([jnp.ones((d, W), F32), a[:ts - d]], axis=0)
            b_sh = jnp.concatenate([jnp.zeros((d, W), F32), b[:ts - d]], axis=0)
        b = a * b_sh + b
        a = a * a_sh
        d *= 2
    h = b + a * hc_ref[...]
    hc_ref[...] = h[ts - 1:ts, :]

    o_ref[0] = (h * jax.nn.gelu(gr_ref[0].astype(F32), approximate=True)).astype(o_ref.dtype)


def _rglru(u3, conv_w, conv_b, wa, ba, wx, bx, lam, W, ts=256):
    B, S, _ = u3.shape
    ts = min(ts, S)
    vec = lambda: pl.BlockSpec((1, W), lambda b, s: (0, 0))
    bd = W // LRU_BLOCKS
    return pl.pallas_call(
        _rglru_kernel,
        grid=(B, S // ts),
        in_specs=[
            pl.BlockSpec((1, ts, W), lambda b, s: (b, s, 0)),
            pl.BlockSpec((1, ts, W), lambda b, s: (b, s, 1)),
            pl.BlockSpec((CONV_WIDTH, W), lambda b, s: (0, 0)),
            vec(),
            pl.BlockSpec((LRU_BLOCKS, bd, bd), lambda b, s: (0, 0, 0)),
            vec(),
            pl.BlockSpec((LRU_BLOCKS, bd, bd), lambda b, s: (0, 0, 0)),
            vec(),
            vec(),
        ],
        out_specs=pl.BlockSpec((1, ts, W), lambda b, s: (b, s, 0)),
        out_shape=jax.ShapeDtypeStruct((B, S, W), BF16),
        scratch_shapes=[pltpu.VMEM((SUBLANES + ts, W), F32), pltpu.VMEM((1, W), F32)],
        compiler_params=_cparams(("parallel", "arbitrary")),
        name="rglru",
    )(u3, u3, conv_w, conv_b, wa, ba, wx, bx, lam)


LOG2E = 1.4426950408889634
MOBA_GROUP = 4
MOBA_HEADS_PER_STEP = 2
MOBA_GATE_CHUNK = 8


def _split3(c):
    c0 = c.astype(BF16).astype(F32)
    r1 = c - c0
    c1 = r1.astype(BF16).astype(F32)
    c2 = (r1 - c1).astype(BF16).astype(F32)
    return c0, c1, c2


def _moba_kernel(slope_ref, q_ref, k_ref, v_ref, o_ref,
                 kaug_ref, vt_ref, kmean_ref, mask_ref, qaug_ref, causal_ref, m_ref, l_ref, acc_ref,
                 za_ref, zb_ref, *, G, HB):
    BLK, dh, EXT = MOBA_BLOCK, HEAD_DIM, LANES
    nb = k_ref.shape[1] // BLK
    j = pl.program_id(2)
    heads = range(HB)
    cols = lambda hh: slice(hh * dh, (hh + 1) * dh)

    @pl.when(j == 0)
    def _():
        lane = lax.broadcasted_iota(jnp.int32, (BLK, EXT), 1)
        rowi = lax.broadcasted_iota(jnp.int32, (BLK, EXT), 0)
        for hh in heads:
            slope = slope_ref[pl.program_id(1) * HB + hh]
            for n in range(nb):
                rows = slice(n * BLK, (n + 1) * BLK)
                kb = k_ref[0, rows, cols(hh)]
                kmean_ref[hh, n:n + 1, :] = jnp.mean(kb.astype(F32), axis=0, keepdims=True)
                c0, c1, c2 = _split3((LOG2E * slope) * (rowi + n * BLK).astype(F32))
                ext = jnp.where(lane == n, 1.0,
                                jnp.where(lane == nb, c0, jnp.where(lane == nb + 1, c1,
                                                                    jnp.where(lane == nb + 2, c2, 0.0))))
                kaug_ref[hh, rows, 0:dh] = kb
                kaug_ref[hh, rows, dh:dh + EXT] = ext.astype(BF16)
                vt_ref[hh, n // G, :, (n % G) * BLK:(n % G + 1) * BLK] = (
                    v_ref[0, rows, cols(hh)].astype(F32).T.astype(BF16))
        kk = lax.broadcasted_iota(jnp.int32, (BLK, BLK), 0)
        qq = lax.broadcasted_iota(jnp.int32, (BLK, BLK), 1)
        causal_ref[...] = jnp.where(kk <= qq, 0.0, NEG_BIG)

        qc = min(MOBA_GATE_CHUNK, nb) * BLK
        blk_i = lax.broadcasted_iota(jnp.int32, (nb, qc), 0)
        lane_i = lax.broadcasted_iota(jnp.int32, (nb, qc), 1)
        for hh in heads:
            for c0 in range(0, nb * BLK, qc):
                qa = q_ref[0, c0:c0 + qc, cols(hh)].astype(F32)
                gate = lax.dot_general(kmean_ref[hh], qa, (((1,), (1,)), ((), ())),
                                       preferred_element_type=F32,
                                       precision=lax.Precision.HIGHEST)
                qpos = lane_i + c0
                past = (blk_i + 1) * BLK <= qpos
                g = jnp.where(past, gate, -jnp.inf)
                sel = jnp.logical_and(blk_i * BLK <= qpos, jnp.logical_not(past))
                for _ in range(min(MOBA_TOPK, nb)):
                    cmax = jnp.max(g, axis=0, keepdims=True)
                    first = jnp.min(jnp.where(g == cmax, blk_i, nb), axis=0, keepdims=True)
                    hit = blk_i == first
                    sel = jnp.logical_or(sel, jnp.logical_and(hit, past))
                    g = jnp.where(hit, -jnp.inf, g)
                mask = jnp.where(sel, 0.0, NEG_BIG)
                for t in range(qc // BLK):
                    mask_ref[hh, c0 // BLK + t] = mask[:, t * BLK:(t + 1) * BLK]

    tail_row = lax.broadcasted_iota(jnp.int32, (EXT - nb, BLK), 0)
    ones3 = jnp.where(tail_row < 3, 1.0, 0.0)
    for hh in heads:
        qf = q_ref[0, pl.ds(pl.multiple_of(j * BLK, BLK), BLK), cols(hh)].astype(F32)
        qaug_ref[hh] = jnp.concatenate([(qf * (HEAD_DIM ** -0.5 * LOG2E)).T, mask_ref[hh, j], ones3],
                                       axis=0).astype(BF16)

    def scores(hh, gi):
        kg = kaug_ref[hh, pl.ds(pl.multiple_of(gi * (G * BLK), G * BLK), G * BLK), :]
        return jnp.dot(kg, qaug_ref[hh], preferred_element_type=F32)

    def update(hh, gi, z):
        m_old = m_ref[hh]
        m_new = jnp.maximum(m_old, jnp.max(z, axis=0, keepdims=True))
        alpha = jnp.exp2(m_old - m_new)
        p = jnp.exp2(z - m_new)
        m_ref[hh] = m_new
        l_ref[hh] = alpha * l_ref[hh] + jnp.sum(p, axis=0, keepdims=True)
        acc_ref[hh] = alpha * acc_ref[hh] + jnp.dot(vt_ref[hh, gi], p.astype(BF16),
                                                    preferred_element_type=F32)

    g_own = j // G
    last = jnp.maximum(g_own - 1, 0)
    in_grp = lax.broadcasted_iota(jnp.int32, (G, BLK, BLK), 0)
    own_mask = jnp.where(in_grp == j % G, causal_ref[...][None], 0.0)
    z_own = [scores(hh, g_own) for hh in heads]
    for hh in heads:
        za_ref[hh] = scores(hh, 0)
    for hh in heads:
        z = (z_own[hh].reshape(G, BLK, BLK) + own_mask).reshape(G * BLK, BLK)
        m0 = jnp.max(z, axis=0, keepdims=True)
        p = jnp.exp2(z - m0)
        m_ref[hh] = m0
        l_ref[hh] = jnp.sum(p, axis=0, keepdims=True)
        acc_ref[hh] = jnp.dot(vt_ref[hh, g_own], p.astype(BF16), preferred_element_type=F32)

    def body(t, _):
        g0 = 2 * t
        for hh in heads:
            zb_ref[hh] = scores(hh, jnp.minimum(g0 + 1, last))
        for hh in heads:
            update(hh, g0, za_ref[hh])

        @pl.when(g0 + 1 < g_own)
        def _():
            for hh in heads:
                za_ref[hh] = scores(hh, jnp.minimum(g0 + 2, last))
            for hh in heads:
                update(hh, g0 + 1, zb_ref[hh])
        return 0

    lax.fori_loop(0, (g_own + 1) // 2, body, 0)
    for hh in heads:
        o_ref[0, :, cols(hh)] = (acc_ref[hh] / l_ref[hh]).T.astype(o_ref.dtype)


def _moba(u3, slopes, q_col, k_col, v_col):
    B, S, _ = u3.shape
    BLK, dh, H, EXT = MOBA_BLOCK, HEAD_DIM, ATTN_HEADS, LANES
    nb = S // BLK
    G = min(MOBA_GROUP, nb)
    HB = MOBA_HEADS_PER_STEP
    assert nb % G == 0 and nb % SUBLANES == 0 and nb + 3 <= EXT
    assert H % HB == 0 and q_col % HB == 0 and k_col % HB == 0 and v_col % HB == 0
    return pl.pallas_call(
        functools.partial(_moba_kernel, G=G, HB=HB),
        grid=(B, H // HB, nb),
        in_specs=[
            pl.BlockSpec(memory_space=pltpu.SMEM),
            pl.BlockSpec((1, S, HB * dh), lambda b, h, j: (b, 0, q_col // HB + h)),
            pl.BlockSpec((1, S, HB * dh), lambda b, h, j: (b, 0, k_col // HB + h)),
            pl.BlockSpec((1, S, HB * dh), lambda b, h, j: (b, 0, v_col // HB + h)),
        ],
        out_specs=pl.BlockSpec((1, BLK, HB * dh), lambda b, h, j: (b, j, h)),
        out_shape=jax.ShapeDtypeStruct((B, S, H * dh), BF16),
        scratch_shapes=[
            pltpu.VMEM((HB, S, dh + EXT), BF16),
            pltpu.VMEM((HB, nb // G, dh, G * BLK), BF16),
            pltpu.VMEM((HB, nb, dh), F32),
            pltpu.VMEM((HB, nb, nb, BLK), F32),
            pltpu.VMEM((HB, dh + EXT, BLK), BF16),
            pltpu.VMEM((BLK, BLK), F32),
            pltpu.VMEM((HB, 1, BLK), F32),
            pltpu.VMEM((HB, 1, BLK), F32),
            pltpu.VMEM((HB, dh, BLK), F32),
            pltpu.VMEM((HB, G * BLK, BLK), F32),
            pltpu.VMEM((HB, G * BLK, BLK), F32),
        ],
        compiler_params=_cparams(("parallel", "parallel", "arbitrary")),
        name="moba",
    )(slopes, u3, u3, u3)


def _merge_kernel(yr_ref, ya_ref, gr0_ref, gr1_ref, ga0_ref, ga1_ref, x_ref,
                  pr_ref, pa_ref, wo_ref, g_ref, wr_ref, br_ref,
                  x1_ref, h2_ref, rt_ref):
    half = gr0_ref.shape[1]
    pr = jnp.dot(yr_ref[...], pr_ref[...], preferred_element_type=F32)
    pa = jnp.dot(ya_ref[...], pa_ref[...], preferred_element_type=F32)
    sig = lambda ref: jax.nn.sigmoid(ref[...].astype(F32))
    m0 = sig(gr0_ref) * pr[:, :half] + sig(ga0_ref) * pa[:, :half]
    m1 = sig(gr1_ref) * pr[:, half:] + sig(ga1_ref) * pa[:, half:]
    merged = jnp.concatenate([m0, m1], axis=-1).astype(BF16)
    x1 = x_ref[...] + jnp.dot(merged, wo_ref[...], preferred_element_type=F32)
    x1_ref[...] = x1
    ms = jnp.mean(x1 * x1, axis=-1, keepdims=True)
    h2 = x1 * lax.rsqrt(ms + RMS_EPS) * g_ref[...]
    _store_token_rows(h2_ref, h2)
    hi = h2.astype(BF16)
    lo = (h2 - hi.astype(F32)).astype(BF16)
    dot = functools.partial(jnp.dot, preferred_element_type=F32)
    rt_ref[...] = _route(dot(hi, wr_ref[0]) + dot(lo, wr_ref[0]) + dot(hi, wr_ref[1]) + br_ref[...])


def _hi_lo(w):
    hi = w.astype(BF16)
    return jnp.stack([hi, (w - hi.astype(F32)).astype(BF16)])


def _merge(y_rec, y_attn, u, gate_col, x2, pr, pa, wo, g, wr, br, tm=256):
    T, D = x2.shape
    W = y_rec.shape[1]
    half = D // 2
    tm = min(tm, T)
    gspec = lambda c: pl.BlockSpec((tm, half), lambda i: (i, gate_col + c))
    row = lambda n: pl.BlockSpec((tm, n), lambda i: (i, 0))
    return pl.pallas_call(
        _merge_kernel,
        grid=(T // tm,),
        in_specs=[
            row(W), row(W), gspec(0), gspec(1), gspec(2), gspec(3), row(D),
            _const_spec(pr.shape), _const_spec(pa.shape), _const_spec(wo.shape),
            _const_spec(g.shape), _const_spec(wr.shape), _const_spec(br.shape),
        ],
        out_specs=[row(D), pl.BlockSpec((tm * _token_rows(D), LANES), lambda i: (i, 0)), row(LANES)],
        out_shape=[
            jax.ShapeDtypeStruct((T, D), F32),
            jax.ShapeDtypeStruct((T * _token_rows(D), LANES), TOKEN_DTYPE),
            jax.ShapeDtypeStruct((T, LANES), F32),
        ],
        compiler_params=_cparams(("parallel",)),
        name="merge",
    )(y_rec, y_attn, u, u, u, u, x2, pr, pa, wo, g, wr, br)


def _route(lg):
    G, EPG = N_GROUPS, EXPERTS_PER_GROUP
    lane = lax.broadcasted_iota(jnp.int32, lg.shape, 1)

    def first_argmax(vals, mask):
        v = jnp.where(mask, vals, -jnp.inf)
        mx = jnp.max(v, axis=-1, keepdims=True)
        idx = jnp.min(jnp.where(jnp.logical_and(mask, v == mx), lane, LANES), axis=-1, keepdims=True)
        return mx, idx

    gmask = lane < G
    gmax, gi = first_argmax(lg, gmask)
    gexp = jnp.where(gmask, jnp.exp(lg - gmax), 0.0)
    grp_w = 1.0 / jnp.sum(gexp, axis=-1, keepdims=True)
    lo = G + gi * EPG
    emask = jnp.logical_and(lane >= lo, lane < lo + EPG)
    emax, i1 = first_argmax(lg, emask)
    eexp = jnp.where(emask, jnp.exp(lg - emax), 0.0)
    esum = jnp.sum(eexp, axis=-1, keepdims=True)
    p1 = 1.0 / esum
    mask2 = jnp.logical_and(emask, lane != i1)
    l2, i2 = first_argmax(lg, mask2)
    p2 = jnp.exp(l2 - emax) / esum
    den = p1 + p2
    w1 = grp_w * (p1 / den)
    w2 = grp_w * (p2 / den)
    e1 = (i1 - G).astype(F32)
    e2 = (i2 - G).astype(F32)
    out = jnp.where(lane == 0, e1, jnp.where(lane == 1, e2, jnp.where(lane == 2, w1, jnp.where(lane == 3, w2, 0.0))))
    return out


def _token_copy(src_hbm, src_tok, dst_ref, dst_tok, sem, c_n):
    return pltpu.make_async_copy(src_hbm.at[pl.ds(pl.multiple_of(src_tok * c_n, c_n), c_n), :],
                                 dst_ref.at[pl.ds(dst_tok * c_n, c_n), :], sem)


def _wait_tokens(src_hbm, dst_ref, sem):
    pltpu.make_async_copy(src_hbm.at[pl.ds(0, dst_ref.shape[0]), :], dst_ref, sem).wait()


def _ffn_kernel(te_ref, cur_ref, nxt_ref, h_hbm, wg_ref, wu_ref, wd_ref, o_ref,
                xbuf_ref, wgb_ref, wub_ref, wdb_ref, sem):
    i = pl.program_id(0)
    n = pl.num_programs(0)
    tr = cur_ref.shape[2]
    c_n = xbuf_ref.shape[1] // tr
    slot = i % 2
    parts = 4

    def gather(idx_ref, s, part):
        for r in range(part * tr // parts, (part + 1) * tr // parts):
            _token_copy(h_hbm, idx_ref[0, 0, r], xbuf_ref.at[s], r, sem.at[s], c_n).start(priority=r % 2)

    @pl.when(i == 0)
    def _():
        for part in range(parts):
            gather(cur_ref, 0, part)

    @pl.when(jnp.logical_or(i == 0, te_ref[i] != te_ref[jnp.maximum(i - 1, 0)]))
    def _():
        wgb_ref[...] = wg_ref[0].astype(BF16)
        wub_ref[...] = wu_ref[0].astype(BF16)
        wdb_ref[...] = wd_ref[0].astype(BF16)

    _wait_tokens(h_hbm, xbuf_ref.at[slot], sem.at[slot])
    x = _load_token_rows(xbuf_ref.at[slot], tr).astype(BF16)
    gather(nxt_ref, 1 - slot, 0)
    g = jnp.dot(x, wgb_ref[...], preferred_element_type=F32)
    gather(nxt_ref, 1 - slot, 1)
    u = jnp.dot(x, wub_ref[...], preferred_element_type=F32)
    hmid = (jax.nn.silu(g) * u).astype(BF16)
    gather(nxt_ref, 1 - slot, 2)
    y = jnp.dot(hmid, wdb_ref[...], preferred_element_type=F32)
    gather(nxt_ref, 1 - slot, 3)
    _store_token_rows(o_ref, y)

    @pl.when(i == n - 1)
    def _():
        _wait_tokens(h_hbm, xbuf_ref.at[1 - slot], sem.at[1 - slot])


def _expert_ffn(h2r, src_tok, tile_e, wg, wu, wd):
    D, Hd = wg.shape[1], wg.shape[2]
    c_n = _token_rows(D)
    n_tiles = tile_e.shape[0]
    idx = src_tok.reshape(n_tiles, 1, ROW_TILE)
    grid_spec = pltpu.PrefetchScalarGridSpec(
        num_scalar_prefetch=1,
        grid=(n_tiles,),
        in_specs=[
            pl.BlockSpec((1, 1, ROW_TILE), lambda i, te: (i, 0, 0), memory_space=pltpu.SMEM),
            pl.BlockSpec((1, 1, ROW_TILE), lambda i, te: (jnp.minimum(i + 1, n_tiles - 1), 0, 0),
                         memory_space=pltpu.SMEM),
            pl.BlockSpec(memory_space=pl.ANY),
            pl.BlockSpec((1, D, Hd), lambda i, te: (te[i], 0, 0)),
            pl.BlockSpec((1, D, Hd), lambda i, te: (te[i], 0, 0)),
            pl.BlockSpec((1, Hd, D), lambda i, te: (te[i], 0, 0)),
        ],
        out_specs=pl.BlockSpec((ROW_TILE * c_n, LANES), lambda i, te: (i, 0)),
        scratch_shapes=[
            pltpu.VMEM((2, ROW_TILE * c_n, LANES), TOKEN_DTYPE),
            pltpu.VMEM((D, Hd), BF16), pltpu.VMEM((D, Hd), BF16), pltpu.VMEM((Hd, D), BF16),
            pltpu.SemaphoreType.DMA((2,)),
        ],
    )
    return pl.pallas_call(
        _ffn_kernel,
        grid_spec=grid_spec,
        out_shape=jax.ShapeDtypeStruct((n_tiles * ROW_TILE * c_n, LANES), TOKEN_DTYPE),
        compiler_params=_cparams(("arbitrary",)),
        name="expert_ffn",
    )(tile_e, idx, idx, h2r, wg, wu, wd)


def _combine_kernel(cur_ref, nxt_ref, x1_ref, rt_ref, g_ref, y_hbm, o_ref, buf_ref, sem):
    i = pl.program_id(0)
    n = pl.num_programs(0)
    tm = x1_ref.shape[0]
    c_n = buf_ref.shape[1] // (2 * tm)
    slot = i % 2

    def gather(pos_ref, s):
        for r in range(tm):
            for k in range(2):
                _token_copy(y_hbm, pos_ref[0, 0, 2 * r + k], buf_ref.at[s], k * tm + r, sem.at[s], c_n).start()

    @pl.when(i == 0)
    def _():
        gather(cur_ref, 0)

    _wait_tokens(y_hbm, buf_ref.at[slot], sem.at[slot])
    gather(nxt_ref, 1 - slot)
    y = _load_token_rows(buf_ref.at[slot], 2 * tm)
    rt = rt_ref[...]
    x = x1_ref[...] + rt[:, 2:3] * y[:tm] + rt[:, 3:4] * y[tm:]
    ms = jnp.mean(x * x, axis=-1, keepdims=True)
    o_ref[...] = x * lax.rsqrt(ms + RMS_EPS) * g_ref[...]

    @pl.when(i == n - 1)
    def _():
        _wait_tokens(y_hbm, buf_ref.at[1 - slot], sem.at[1 - slot])


def _combine(x1, routed, pos, g, y, tm=512):
    T, D = x1.shape
    tm = min(tm, T)
    c_n = _token_rows(D)
    n = T // tm
    idx = pos.reshape(n, 1, 2 * tm)
    return pl.pallas_call(
        _combine_kernel,
        grid=(n,),
        in_specs=[
            pl.BlockSpec((1, 1, 2 * tm), lambda i: (i, 0, 0), memory_space=pltpu.SMEM),
            pl.BlockSpec((1, 1, 2 * tm), lambda i: (jnp.minimum(i + 1, n - 1), 0, 0),
                         memory_space=pltpu.SMEM),
            pl.BlockSpec((tm, D), lambda i: (i, 0)),
            pl.BlockSpec((tm, LANES), lambda i: (i, 0)),
            pl.BlockSpec((1, D), lambda i: (0, 0)),
            pl.BlockSpec(memory_space=pl.ANY),
        ],
        out_specs=pl.BlockSpec((tm, D), lambda i: (i, 0)),
        out_shape=jax.ShapeDtypeStruct((T, D), F32),
        scratch_shapes=[pltpu.VMEM((2, 2 * tm * c_n, LANES), TOKEN_DTYPE), pltpu.SemaphoreType.DMA((2,))],
        compiler_params=_cparams(("arbitrary",)),
        name="combine",
    )(idx, idx, x1, routed, g, y)


def _sorted_layout(eid, n_tiles):
    E, TR = N_EXPERTS, ROW_TILE
    eflat = eid.reshape(-1)
    A = eflat.shape[0]
    onehot = (eflat[:, None] == jnp.arange(E, dtype=jnp.int32)[None, :]).astype(jnp.int32)
    csum = jnp.cumsum(onehot, axis=0)
    counts = csum[-1]
    ptiles = (counts + TR - 1) // TR
    tile_end = jnp.cumsum(ptiles)
    pad_off = (tile_end - ptiles) * TR
    pos = jnp.sum(onehot * (csum - 1 + pad_off[None, :]), axis=1)
    assert n_tiles * TR == A + E * TR
    k = jnp.arange(TR, dtype=jnp.int32)[None, :]
    n_pad = (ptiles * TR - counts)[:, None]
    pad_key = jnp.where(k < n_pad, (pad_off + counts)[:, None] + k, n_tiles * TR)
    keys = jnp.concatenate([pos, pad_key.reshape(-1)])
    toks = jnp.concatenate([jnp.arange(A, dtype=jnp.int32) // 2, jnp.zeros((E * TR,), jnp.int32)])
    _, src_tok = lax.sort_key_val(keys, toks)
    tiles = jnp.arange(n_tiles, dtype=jnp.int32)
    tile_e_raw = jnp.sum((tile_end[None, :] <= tiles[:, None]).astype(jnp.int32), axis=1)
    tile_e = jnp.minimum(tile_e_raw, E - 1)
    return src_tok, pos.astype(jnp.int32), tile_e


def kernel(x, norm_attn_g, w_in, conv_w, conv_b, lru_wa, lru_ba, lru_wx, lru_bx, lru_lambda, proj_rec, proj_attn, w_out, norm_ffn_g, router_group_w, router_group_b, router_expert_w, router_expert_b, expert_w_gate, expert_w_up, expert_w_down, norm_final_g):
    B, S, D = x.shape
    T = B * S
    depth = w_in.shape[0]
    W = lru_lambda.shape[1]
    AW = ATTN_HEADS * HEAD_DIM
    assert W % HEAD_DIM == 0 and W == AW and S % MOBA_BLOCK == 0
    q_col = 2 * W // HEAD_DIM
    k_col = q_col + ATTN_HEADS
    v_col = k_col + ATTN_HEADS
    gate_col = (2 * W + 3 * AW) // (D // 2)
    assert gate_col * (D // 2) == 2 * W + 3 * AW
    slopes = 2.0 ** (-8.0 * jnp.arange(1, ATTN_HEADS + 1, dtype=F32) / ATTN_HEADS)
    assert (2 * T) % ROW_TILE == 0
    n_tiles = 2 * T // ROW_TILE + N_EXPERTS
    row = lambda v: v.reshape(1, -1)

    assert depth == 1, "the final norm is fused into the single layer's combine"
    l = 0
    x2 = x.reshape(T, D)
    u = _inproj(x2, row(norm_attn_g[l]), w_in[l].astype(BF16))
    u3 = u.reshape(B, S, -1)
    y_rec = _rglru(u3, conv_w[l], row(conv_b[l]), lru_wa[l].astype(BF16), row(lru_ba[l]),
                   lru_wx[l].astype(BF16), row(lru_bx[l]), row(lru_lambda[l]), W)
    y_attn = _moba(u3, slopes, q_col, k_col, v_col)
    wr = jnp.zeros((D, LANES), F32)
    wr = wr.at[:, :N_GROUPS].set(router_group_w[l]).at[:, N_GROUPS:N_GROUPS + N_EXPERTS].set(router_expert_w[l])
    br = jnp.zeros((1, LANES), F32)
    br = br.at[0, :N_GROUPS].set(router_group_b[l]).at[0, N_GROUPS:N_GROUPS + N_EXPERTS].set(router_expert_b[l])
    x1, h2, routed = _merge(y_rec.reshape(T, W), y_attn.reshape(T, AW), u, gate_col, x2,
                            proj_rec[l].astype(BF16), proj_attn[l].astype(BF16),
                            w_out[l].astype(BF16), row(norm_ffn_g[l]), _hi_lo(wr), br)
    eid = routed[:, :2].astype(jnp.int32)
    src_tok, pos, tile_e = _sorted_layout(eid, n_tiles)
    y = _expert_ffn(h2, src_tok, tile_e, expert_w_gate[l], expert_w_up[l], expert_w_down[l])
    out = _combine(x1, routed, pos, row(norm_final_g), y)
    return out.reshape(B, S, D)
```

```python
import functools

import jax
import jax.numpy as jnp
from jax import lax
from jax.experimental import pallas as pl
from jax.experimental.pallas import tpu as pltpu

F32 = jnp.float32
BF16 = jnp.bfloat16

RMS_EPS = 1e-6
LRU_BLOCKS = 8
CONV_WIDTH = 4
LRU_C = 8.0
ATTN_HEADS = 8
HEAD_DIM = 128
MOBA_BLOCK = 256
MOBA_TOPK = 3
N_GROUPS = 4
EXPERTS_PER_GROUP = 8
N_EXPERTS = N_GROUPS * EXPERTS_PER_GROUP

LANES = 128
SUBLANES = 8
NEG_BIG = -1e30
ROW_TILE = 256
VMEM_LIMIT = 56 * 1024 * 1024


def _cparams(sem):
    return pltpu.CompilerParams(dimension_semantics=sem, vmem_limit_bytes=VMEM_LIMIT)


def _const_spec(shape):
    nd = len(shape)
    return pl.BlockSpec(shape, lambda *_: (0,) * nd, pipeline_mode=pl.Buffered(1))


TOKEN_DTYPE = F32


def _token_rows(d):
    return d // LANES


def _store_token_rows(ref, val):
    n, d = val.shape
    c_n = _token_rows(d)
    for c in range(c_n):
        ref[pl.ds(c, n, stride=c_n), :] = val[:, c * LANES:(c + 1) * LANES]


def _load_token_rows(ref, n):
    c_n = ref.shape[0] // n
    return jnp.concatenate([ref[pl.ds(c, n, stride=c_n), :] for c in range(c_n)], axis=1)


def _inproj_kernel(x_ref, g_ref, w_ref, o_ref, h_ref, *, chunk):
    tm = x_ref.shape[0]

    @pl.when(pl.program_id(1) == 0)
    def _():
        def body(c, _):
            r0 = pl.multiple_of(c * chunk, chunk)
            x = x_ref[pl.ds(r0, chunk), :]
            ms = jnp.mean(x * x, axis=-1, keepdims=True)
            h_ref[pl.ds(r0, chunk), :] = (x * lax.rsqrt(ms + RMS_EPS) * g_ref[...]).astype(BF16)
            return 0
        lax.fori_loop(0, tm // chunk, body, 0)

    o_ref[...] = jnp.dot(h_ref[...], w_ref[...], preferred_element_type=F32).astype(o_ref.dtype)


def _inproj(x2, g, w_bf16, tm=1024, tn=2304):
    T, D = x2.shape
    N = w_bf16.shape[1]
    tm = min(tm, T)
    assert N % tn == 0
    return pl.pallas_call(
        functools.partial(_inproj_kernel, chunk=min(256, tm)),
        grid=(T // tm, N // tn),
        in_specs=[
            pl.BlockSpec((tm, D), lambda i, j: (i, 0)),
            pl.BlockSpec((1, D), lambda i, j: (0, 0)),
            pl.BlockSpec((D, tn), lambda i, j: (0, j)),
        ],
        out_specs=pl.BlockSpec((tm, tn), lambda i, j: (i, j)),
        out_shape=jax.ShapeDtypeStruct((T, N), BF16),
        scratch_shapes=[pltpu.VMEM((tm, D), BF16)],
        compiler_params=_cparams(("parallel", "arbitrary")),
        name="inproj",
    )(x2, g, w_bf16)


def _rglru_kernel(xr_ref, gr_ref, cw_ref, cb_ref, wa_ref, ba_ref, wx_ref, bx_ref, lam_ref,
                  o_ref, xbuf_ref, hc_ref):
    ts = xr_ref.shape[1]
    W = xr_ref.shape[2]
    bd = W // LRU_BLOCKS
    pad = SUBLANES

    @pl.when(pl.program_id(1) == 0)
    def _():
        xbuf_ref[0:pad, :] = jnp.zeros((pad, W), F32)
        hc_ref[...] = jnp.zeros_like(hc_ref)

    xbuf_ref[pad:pad + ts, :] = xr_ref[0].astype(F32)
    xc = cb_ref[...] + cw_ref[0:1, :] * xbuf_ref[pl.ds(pad - 3, ts), :]
    for k in range(1, CONV_WIDTH):
        xc = xc + cw_ref[k:k + 1, :] * xbuf_ref[pl.ds(pad - 3 + k, ts), :]
    xbuf_ref[0:pad, :] = xbuf_ref[ts:ts + pad, :]

    xcb = xc.astype(BF16)
    ra, ri = [], []
    for n in range(LRU_BLOCKS):
        blk = xcb[:, n * bd:(n + 1) * bd]
        ra.append(jnp.dot(blk, wa_ref[n], preferred_element_type=F32))
        ri.append(jnp.dot(blk, wx_ref[n], preferred_element_type=F32))
    r = jax.nn.sigmoid(jnp.concatenate(ra, axis=-1) + ba_ref[...])
    i = jax.nn.sigmoid(jnp.concatenate(ri, axis=-1) + bx_ref[...])
    lam = lam_ref[...]
    log_sig = jnp.minimum(lam, 0.0) - jnp.log1p(jnp.exp(-jnp.abs(lam)))
    log_a = LRU_C * r * log_sig
    a = jnp.exp(log_a)
    th = jnp.tanh(-log_a)
    b = jnp.sqrt(2.0 * th / (1.0 + th)) * (i * xc)

    row = lax.broadcasted_iota(jnp.int32, (ts, W), 0)
    d = 1
    while d < ts:
        if d % SUBLANES:
            keep = row >= d
            a_sh = jnp.where(keep, pltpu.roll(a, d, 0), 1.0)
            b_sh = jnp.where(keep, pltpu.roll(b, d, 0), 0.0)
        else:
            a_sh = jnp.concatenate([jnp.ones((d, W), F32), a[:ts - d]], axis=0)
            b_sh = jnp.concatenate([jnp.zeros((d, W), F32), b[:ts - d]], axis=0)
        b = a * b_sh + b
        a = a * a_sh
        d *= 2
    h = b + a * hc_ref[...]
    hc_ref[...] = h[ts - 1:ts, :]

    o_ref[0] = (h * jax.nn.gelu(gr_ref[0].astype(F32), approximate=True)).astype(o_ref.dtype)


def _rglru(u3, conv_w, conv_b, wa, ba, wx, bx, lam, W, ts=256):
    B, S, _ = u3.shape
    ts = min(ts, S)
    vec = lambda: pl.BlockSpec((1, W), lambda b, s: (0, 0))
    bd = W // LRU_BLOCKS
    return pl.pallas_call(
        _rglru_kernel,
        grid=(B, S // ts),
        in_specs=[
            pl.BlockSpec((1, ts, W), lambda b, s: (b, s, 0)),
            pl.BlockSpec((1, ts, W), lambda b, s: (b, s, 1)),
            pl.BlockSpec((CONV_WIDTH, W), lambda b, s: (0, 0)),
            vec(),
            pl.BlockSpec((LRU_BLOCKS, bd, bd), lambda b, s: (0, 0, 0)),
            vec(),
            pl.BlockSpec((LRU_BLOCKS, bd, bd), lambda b, s: (0, 0, 0)),
            vec(),
            vec(),
        ],
        out_specs=pl.BlockSpec((1, ts, W), lambda b, s: (b, s, 0)),
        out_shape=jax.ShapeDtypeStruct((B, S, W), BF16),
        scratch_shapes=[pltpu.VMEM((SUBLANES + ts, W), F32), pltpu.VMEM((1, W), F32)],
        compiler_params=_cparams(("parallel", "arbitrary")),
        name="rglru",
    )(u3, u3, conv_w, conv_b, wa, ba, wx, bx, lam)


LOG2E = 1.4426950408889634
MOBA_GROUP = 4
MOBA_HEADS_PER_STEP = 2
MOBA_GATE_CHUNK = 8


def _split3(c):
    c0 = c.astype(BF16).astype(F32)
    r1 = c - c0
    c1 = r1.astype(BF16).astype(F32)
    c2 = (r1 - c1).astype(BF16).astype(F32)
    return c0, c1, c2


def _moba_kernel(slope_ref, q_ref, k_ref, v_ref, o_ref,
                 kaug_ref, vt_ref, kmean_ref, mask_ref, qaug_ref, causal_ref, m_ref, l_ref, acc_ref,
                 za_ref, zb_ref, *, G, HB):
    BLK, dh, EXT = MOBA_BLOCK, HEAD_DIM, LANES
    nb = k_ref.shape[1] // BLK
    j = pl.program_id(2)
    heads = range(HB)
    cols = lambda hh: slice(hh * dh, (hh + 1) * dh)

    @pl.when(j == 0)
    def _():
        lane = lax.broadcasted_iota(jnp.int32, (BLK, EXT), 1)
        rowi = lax.broadcasted_iota(jnp.int32, (BLK, EXT), 0)
        for hh in heads:
            slope = slope_ref[pl.program_id(1) * HB + hh]
            for n in range(nb):
                rows = slice(n * BLK, (n + 1) * BLK)
                kb = k_ref[0, rows, cols(hh)]
                kmean_ref[hh, n:n + 1, :] = jnp.mean(kb.astype(F32), axis=0, keepdims=True)
                c0, c1, c2 = _split3((LOG2E * slope) * (rowi + n * BLK).astype(F32))
                ext = jnp.where(lane == n, 1.0,
                                jnp.where(lane == nb, c0, jnp.where(lane == nb + 1, c1,
                                                                    jnp.where(lane == nb + 2, c2, 0.0))))
                kaug_ref[hh, rows, 0:dh] = kb
                kaug_ref[hh, rows, dh:dh + EXT] = ext.astype(BF16)
                vt_ref[hh, n // G, :, (n % G) * BLK:(n % G + 1) * BLK] = (
                    v_ref[0, rows, cols(hh)].astype(F32).T.astype(BF16))
        kk = lax.broadcasted_iota(jnp.int32, (BLK, BLK), 0)
        qq = lax.broadcasted_iota(jnp.int32, (BLK, BLK), 1)
        causal_ref[...] = jnp.where(kk <= qq, 0.0, NEG_BIG)

        qc = min(MOBA_GATE_CHUNK, nb) * BLK
        blk_i = lax.broadcasted_iota(jnp.int32, (nb, qc), 0)
        lane_i = lax.broadcasted_iota(jnp.int32, (nb, qc), 1)
        for hh in heads:
            for c0 in range(0, nb * BLK, qc):
                qa = q_ref[0, c0:c0 + qc, cols(hh)].astype(F32)
                gate = lax.dot_general(kmean_ref[hh], qa, (((1,), (1,)), ((), ())),
                                       preferred_element_type=F32,
                                       precision=lax.Precision.HIGHEST)
                qpos = lane_i + c0
                past = (blk_i + 1) * BLK <= qpos
                g = jnp.where(past, gate, -jnp.inf)
                sel = jnp.logical_and(blk_i * BLK <= qpos, jnp.logical_not(past))
                for _ in range(min(MOBA_TOPK, nb)):
                    cmax = jnp.max(g, axis=0, keepdims=True)
                    first = jnp.min(jnp.where(g == cmax, blk_i, nb), axis=0, keepdims=True)
                    hit = blk_i == first
                    sel = jnp.logical_or(sel, jnp.logical_and(hit, past))
                    g = jnp.where(hit, -jnp.inf, g)
                mask = jnp.where(sel, 0.0, NEG_BIG)
                for t in range(qc // BLK):
                    mask_ref[hh, c0 // BLK + t] = mask[:, t * BLK:(t + 1) * BLK]

    tail_row = lax.broadcasted_iota(jnp.int32, (EXT - nb, BLK), 0)
    ones3 = jnp.where(tail_row < 3, 1.0, 0.0)
    for hh in heads:
        qf = q_ref[0, pl.ds(pl.multiple_of(j * BLK, BLK), BLK), cols(hh)].astype(F32)
        qaug_ref[hh] = jnp.concatenate([(qf * (HEAD_DIM ** -0.5 * LOG2E)).T, mask_ref[hh, j], ones3],
                                       axis=0).astype(BF16)

    def scores(hh, gi):
        kg = kaug_ref[hh, pl.ds(pl.multiple_of(gi * (G * BLK), G * BLK), G * BLK), :]
        return jnp.dot(kg, qaug_ref[hh], preferred_element_type=F32)

    def update(hh, gi, z):
        m_old = m_ref[hh]
        m_new = jnp.maximum(m_old, jnp.max(z, axis=0, keepdims=True))
        alpha = jnp.exp2(m_old - m_new)
        p = jnp.exp2(z - m_new)
        m_ref[hh] = m_new
        l_ref[hh] = alpha * l_ref[hh] + jnp.sum(p, axis=0, keepdims=True)
        acc_ref[hh] = alpha * acc_ref[hh] + jnp.dot(vt_ref[hh, gi], p.astype(BF16),
                                                    preferred_element_type=F32)

    g_own = j // G
    last = jnp.maximum(g_own - 1, 0)
    in_grp = lax.broadcasted_iota(jnp.int32, (G, BLK, BLK), 0)
    own_mask = jnp.where(in_grp == j % G, causal_ref[...][None], 0.0)
    z_own = [scores(hh, g_own) for hh in heads]
    for hh in heads:
        za_ref[hh] = scores(hh, 0)
    for hh in heads:
        z = (z_own[hh].reshape(G, BLK, BLK) + own_mask).reshape(G * BLK, BLK)
        m0 = jnp.max(z, axis=0, keepdims=True)
        p = jnp.exp2(z - m0)
        m_ref[hh] = m0
        l_ref[hh] = jnp.sum(p, axis=0, keepdims=True)
        acc_ref[hh] = jnp.dot(vt_ref[hh, g_own], p.astype(BF16), preferred_element_type=F32)

    def body(t, _):
        g0 = 2 * t
        for hh in heads:
            zb_ref[hh] = scores(hh, jnp.minimum(g0 + 1, last))
        for hh in heads:
            update(hh, g0, za_ref[hh])

        @pl.when(g0 + 1 < g_own)
        def _():
            for hh in heads:
                za_ref[hh] = scores(hh, jnp.minimum(g0 + 2, last))
            for hh in heads:
                update(hh, g0 + 1, zb_ref[hh])
        return 0

    lax.fori_loop(0, (g_own + 1) // 2, body, 0)
    for hh in heads:
        o_ref[0, :, cols(hh)] = (acc_ref[hh] / l_ref[hh]).T.astype(o_ref.dtype)


def _moba(u3, slopes, q_col, k_col, v_col):
    B, S, _ = u3.shape
    BLK, dh, H, EXT = MOBA_BLOCK, HEAD_DIM, ATTN_HEADS, LANES
    nb = S // BLK
    G = min(MOBA_GROUP, nb)
    HB = MOBA_HEADS_PER_STEP
    assert nb % G == 0 and nb % SUBLANES == 0 and nb + 3 <= EXT
    assert H % HB == 0 and q_col % HB == 0 and k_col % HB == 0 and v_col % HB == 0
    return pl.pallas_call(
        functools.partial(_moba_kernel, G=G, HB=HB),
        grid=(B, H // HB, nb),
        in_specs=[
            pl.BlockSpec(memory_space=pltpu.SMEM),
            pl.BlockSpec((1, S, HB * dh), lambda b, h, j: (b, 0, q_col // HB + h)),
            pl.BlockSpec((1, S, HB * dh), lambda b, h, j: (b, 0, k_col // HB + h)),
            pl.BlockSpec((1, S, HB * dh), lambda b, h, j: (b, 0, v_col // HB + h)),
        ],
        out_specs=pl.BlockSpec((1, BLK, HB * dh), lambda b, h, j: (b, j, h)),
        out_shape=jax.ShapeDtypeStruct((B, S, H * dh), BF16),
        scratch_shapes=[
            pltpu.VMEM((HB, S, dh + EXT), BF16),
            pltpu.VMEM((HB, nb // G, dh, G * BLK), BF16),
            pltpu.VMEM((HB, nb, dh), F32),
            pltpu.VMEM((HB, nb, nb, BLK), F32),
            pltpu.VMEM((HB, dh + EXT, BLK), BF16),
            pltpu.VMEM((BLK, BLK), F32),
            pltpu.VMEM((HB, 1, BLK), F32),
            pltpu.VMEM((HB, 1, BLK), F32),
            pltpu.VMEM((HB, dh, BLK), F32),
            pltpu.VMEM((HB, G * BLK, BLK), F32),
            pltpu.VMEM((HB, G * BLK, BLK), F32),
        ],
        compiler_params=_cparams(("parallel", "parallel", "arbitrary")),
        name="moba",
    )(slopes, u3, u3, u3)


def _merge_kernel(yr_ref, ya_ref, gr0_ref, gr1_ref, ga0_ref, ga1_ref, x_ref,
                  pr_ref, pa_ref, wo_ref, g_ref, wr_ref, br_ref,
                  x1_ref, h2_ref, rt_ref):
    half = gr0_ref.shape[1]
    pr = jnp.dot(yr_ref[...], pr_ref[...], preferred_element_type=F32)
    pa = jnp.dot(ya_ref[...], pa_ref[...], preferred_element_type=F32)
    sig = lambda ref: jax.nn.sigmoid(ref[...].astype(F32))
    m0 = sig(gr0_ref) * pr[:, :half] + sig(ga0_ref) * pa[:, :half]
    m1 = sig(gr1_ref) * pr[:, half:] + sig(ga1_ref) * pa[:, half:]
    merged = jnp.concatenate([m0, m1], axis=-1).astype(BF16)
    x1 = x_ref[...] + jnp.dot(merged, wo_ref[...], preferred_element_type=F32)
    x1_ref[...] = x1
    ms = jnp.mean(x1 * x1, axis=-1, keepdims=True)
    h2 = x1 * lax.rsqrt(ms + RMS_EPS) * g_ref[...]
    _store_token_rows(h2_ref, h2)
    hi = h2.astype(BF16)
    lo = (h2 - hi.astype(F32)).astype(BF16)
    dot = functools.partial(jnp.dot, preferred_element_type=F32)
    rt_ref[...] = _route(dot(hi, wr_ref[0]) + dot(lo, wr_ref[0]) + dot(hi, wr_ref[1]) + br_ref[...])


def _hi_lo(w):
    hi = w.astype(BF16)
    return jnp.stack([hi, (w - hi.astype(F32)).astype(BF16)])


def _merge(y_rec, y_attn, u, gate_col, x2, pr, pa, wo, g, wr, br, tm=256):
    T, D = x2.shape
    W = y_rec.shape[1]
    half = D // 2
    tm = min(tm, T)
    gspec = lambda c: pl.BlockSpec((tm, half), lambda i: (i, gate_col + c))
    row = lambda n: pl.BlockSpec((tm, n), lambda i: (i, 0))
    return pl.pallas_call(
        _merge_kernel,
        grid=(T // tm,),
        in_specs=[
            row(W), row(W), gspec(0), gspec(1), gspec(2), gspec(3), row(D),
            _const_spec(pr.shape), _const_spec(pa.shape), _const_spec(wo.shape),
            _const_spec(g.shape), _const_spec(wr.shape), _const_spec(br.shape),
        ],
        out_specs=[row(D), pl.BlockSpec((tm * _token_rows(D), LANES), lambda i: (i, 0)), row(LANES)],
        out_shape=[
            jax.ShapeDtypeStruct((T, D), F32),
            jax.ShapeDtypeStruct((T * _token_rows(D), LANES), TOKEN_DTYPE),
            jax.ShapeDtypeStruct((T, LANES), F32),
        ],
        compiler_params=_cparams(("parallel",)),
        name="merge",
    )(y_rec, y_attn, u, u, u, u, x2, pr, pa, wo, g, wr, br)


def _route(lg):
    G, EPG = N_GROUPS, EXPERTS_PER_GROUP
    lane = lax.broadcasted_iota(jnp.int32, lg.shape, 1)

    def first_argmax(vals, mask):
        v = jnp.where(mask, vals, -jnp.inf)
        mx = jnp.max(v, axis=-1, keepdims=True)
        idx = jnp.min(jnp.where(jnp.logical_and(mask, v == mx), lane, LANES), axis=-1, keepdims=True)
        return mx, idx

    gmask = lane < G
    gmax, gi = first_argmax(lg, gmask)
    gexp = jnp.where(gmask, jnp.exp(lg - gmax), 0.0)
    grp_w = 1.0 / jnp.sum(gexp, axis=-1, keepdims=True)
    lo = G + gi * EPG
    emask = jnp.logical_and(lane >= lo, lane < lo + EPG)
    emax, i1 = first_argmax(lg, emask)
    eexp = jnp.where(emask, jnp.exp(lg - emax), 0.0)
    esum = jnp.sum(eexp, axis=-1, keepdims=True)
    p1 = 1.0 / esum
    mask2 = jnp.logical_and(emask, lane != i1)
    l2, i2 = first_argmax(lg, mask2)
    p2 = jnp.exp(l2 - emax) / esum
    den = p1 + p2
    w1 = grp_w * (p1 / den)
    w2 = grp_w * (p2 / den)
    e1 = (i1 - G).astype(F32)
    e2 = (i2 - G).astype(F32)
    out = jnp.where(lane == 0, e1, jnp.where(lane == 1, e2, jnp.where(lane == 2, w1, jnp.where(lane == 3, w2, 0.0))))
    return out


def _token_copy(src_hbm, src_tok, dst_ref, dst_tok, sem, c_n):
    return pltpu.make_async_copy(src_hbm.at[pl.ds(pl.multiple_of(src_tok * c_n, c_n), c_n), :],
                                 dst_ref.at[pl.ds(dst_tok * c_n, c_n), :], sem)


def _wait_tokens(src_hbm, dst_ref, sem):
    pltpu.make_async_copy(src_hbm.at[pl.ds(0, dst_ref.shape[0]), :], dst_ref, sem).wait()


def _ffn_kernel(te_ref, cur_ref, nxt_ref, h_hbm, wg_ref, wu_ref, wd_ref, o_ref,
                xbuf_ref, wgu_ref, wdb_ref, sem):
    i = pl.program_id(0)
    n = pl.num_programs(0)
    tr = cur_ref.shape[2]
    c_n = xbuf_ref.shape[1] // tr
    slot = i % 2
    parts = 4

    def gather(idx_ref, s, part):
        for r in range(part * tr // parts, (part + 1) * tr // parts):
            _token_copy(h_hbm, idx_ref[0, 0, r], xbuf_ref.at[s], r, sem.at[s], c_n).start(priority=r % 2)

    @pl.when(i == 0)
    def _():
        for part in range(parts):
            gather(cur_ref, 0, part)

    @pl.when(jnp.logical_or(i == 0, te_ref[i] != te_ref[jnp.maximum(i - 1, 0)]))
    def _():
        hd = wg_ref.shape[2]
        wgu_ref[:, 0:hd] = wg_ref[0].astype(BF16)
        wgu_ref[:, hd:2 * hd] = wu_ref[0].astype(BF16)
        wdb_ref[...] = wd_ref[0].astype(BF16)

    _wait_tokens(h_hbm, xbuf_ref.at[slot], sem.at[slot])
    x = _load_token_rows(xbuf_ref.at[slot], tr).astype(BF16)
    gather(nxt_ref, 1 - slot, 0)
    gu = jnp.dot(x, wgu_ref[...], preferred_element_type=F32)
    gather(nxt_ref, 1 - slot, 1)
    hd = gu.shape[1] // 2
    hmid = (jax.nn.silu(gu[:, :hd]) * gu[:, hd:]).astype(BF16)
    gather(nxt_ref, 1 - slot, 2)
    y = jnp.dot(hmid, wdb_ref[...], preferred_element_type=F32)
    gather(nxt_ref, 1 - slot, 3)
    _store_token_rows(o_ref, y)

    @pl.when(i == n - 1)
    def _():
        _wait_tokens(h_hbm, xbuf_ref.at[1 - slot], sem.at[1 - slot])


def _expert_ffn(h2r, src_tok, tile_e, wg, wu, wd):
    D, Hd = wg.shape[1], wg.shape[2]
    c_n = _token_rows(D)
    n_tiles = tile_e.shape[0]
    idx = src_tok.reshape(n_tiles, 1, ROW_TILE)
    grid_spec = pltpu.PrefetchScalarGridSpec(
        num_scalar_prefetch=1,
        grid=(n_tiles,),
        in_specs=[
            pl.BlockSpec((1, 1, ROW_TILE), lambda i, te: (i, 0, 0), memory_space=pltpu.SMEM),
            pl.BlockSpec((1, 1, ROW_TILE), lambda i, te: (jnp.minimum(i + 1, n_tiles - 1), 0, 0),
                         memory_space=pltpu.SMEM),
            pl.BlockSpec(memory_space=pl.ANY),
            pl.BlockSpec((1, D, Hd), lambda i, te: (te[i], 0, 0)),
            pl.BlockSpec((1, D, Hd), lambda i, te: (te[i], 0, 0)),
            pl.BlockSpec((1, Hd, D), lambda i, te: (te[i], 0, 0)),
        ],
        out_specs=pl.BlockSpec((ROW_TILE * c_n, LANES), lambda i, te: (i, 0)),
        scratch_shapes=[
            pltpu.VMEM((2, ROW_TILE * c_n, LANES), TOKEN_DTYPE),
            pltpu.VMEM((D, 2 * Hd), BF16), pltpu.VMEM((Hd, D), BF16),
            pltpu.SemaphoreType.DMA((2,)),
        ],
    )
    return pl.pallas_call(
        _ffn_kernel,
        grid_spec=grid_spec,
        out_shape=jax.ShapeDtypeStruct((n_tiles * ROW_TILE * c_n, LANES), TOKEN_DTYPE),
        compiler_params=_cparams(("arbitrary",)),
        name="expert_ffn",
    )(tile_e, idx, idx, h2r, wg, wu, wd)


def _combine_kernel(cur_ref, nxt_ref, x1_ref, rt_ref, g_ref, y_hbm, o_ref, buf_ref, sem):
    i = pl.program_id(0)
    n = pl.num_programs(0)
    tm = x1_ref.shape[0]
    c_n = buf_ref.shape[1] // (2 * tm)
    slot = i % 2

    def gather(pos_ref, s):
        for r in range(tm):
            for k in range(2):
                _token_copy(y_hbm, pos_ref[0, 0, 2 * r + k], buf_ref.at[s], k * tm + r, sem.at[s], c_n).start()

    @pl.when(i == 0)
    def _():
        gather(cur_ref, 0)

    _wait_tokens(y_hbm, buf_ref.at[slot], sem.at[slot])
    gather(nxt_ref, 1 - slot)
    y = _load_token_rows(buf_ref.at[slot], 2 * tm)
    rt = rt_ref[...]
    x = x1_ref[...] + rt[:, 2:3] * y[:tm] + rt[:, 3:4] * y[tm:]
    ms = jnp.mean(x * x, axis=-1, keepdims=True)
    o_ref[...] = x * lax.rsqrt(ms + RMS_EPS) * g_ref[...]

    @pl.when(i == n - 1)
    def _():
        _wait_tokens(y_hbm, buf_ref.at[1 - slot], sem.at[1 - slot])


def _combine(x1, routed, pos, g, y, tm=512):
    T, D = x1.shape
    tm = min(tm, T)
    c_n = _token_rows(D)
    n = T // tm
    idx = pos.reshape(n, 1, 2 * tm)
    return pl.pallas_call(
        _combine_kernel,
        grid=(n,),
        in_specs=[
            pl.BlockSpec((1, 1, 2 * tm), lambda i: (i, 0, 0), memory_space=pltpu.SMEM),
            pl.BlockSpec((1, 1, 2 * tm), lambda i: (jnp.minimum(i + 1, n - 1), 0, 0),
                         memory_space=pltpu.SMEM),
            pl.BlockSpec((tm, D), lambda i: (i, 0)),
            pl.BlockSpec((tm, LANES), lambda i: (i, 0)),
            pl.BlockSpec((1, D), lambda i: (0, 0)),
            pl.BlockSpec(memory_space=pl.ANY),
        ],
        out_specs=pl.BlockSpec((tm, D), lambda i: (i, 0)),
        out_shape=jax.ShapeDtypeStruct((T, D), F32),
        scratch_shapes=[pltpu.VMEM((2, 2 * tm * c_n, LANES), TOKEN_DTYPE), pltpu.SemaphoreType.DMA((2,))],
        compiler_params=_cparams(("arbitrary",)),
        name="combine",
    )(idx, idx, x1, routed, g, y)


def _sorted_layout(eid, n_tiles):
    E, TR = N_EXPERTS, ROW_TILE
    eflat = eid.reshape(-1)
    A = eflat.shape[0]
    onehot = (eflat[:, None] == jnp.arange(E, dtype=jnp.int32)[None, :]).astype(jnp.int32)
    csum = jnp.cumsum(onehot, axis=0)
    counts = csum[-1]
    ptiles = (counts + TR - 1) // TR
    tile_end = jnp.cumsum(ptiles)
    pad_off = (tile_end - ptiles) * TR
    pos = jnp.sum(onehot * (csum - 1 + pad_off[None, :]), axis=1)
    assert n_tiles * TR == A + E * TR
    k = jnp.arange(TR, dtype=jnp.int32)[None, :]
    n_pad = (ptiles * TR - counts)[:, None]
    pad_key = jnp.where(k < n_pad, (pad_off + counts)[:, None] + k, n_tiles * TR)
    keys = jnp.concatenate([pos, pad_key.reshape(-1)])
    toks = jnp.concatenate([jnp.arange(A, dtype=jnp.int32) // 2, jnp.zeros((E * TR,), jnp.int32)])
    _, src_tok = lax.sort_key_val(keys, toks)
    tiles = jnp.arange(n_tiles, dtype=jnp.int32)
    tile_e_raw = jnp.sum((tile_end[None, :] <= tiles[:, None]).astype(jnp.int32), axis=1)
    tile_e = jnp.minimum(tile_e_raw, E - 1)
    return src_tok, pos.astype(jnp.int32), tile_e


def kernel(x, norm_attn_g, w_in, conv_w, conv_b, lru_wa, lru_ba, lru_wx, lru_bx, lru_lambda, proj_rec, proj_attn, w_out, norm_ffn_g, router_group_w, router_group_b, router_expert_w, router_expert_b, expert_w_gate, expert_w_up, expert_w_down, norm_final_g):
    B, S, D = x.shape
    T = B * S
    depth = w_in.shape[0]
    W = lru_lambda.shape[1]
    AW = ATTN_HEADS * HEAD_DIM
    assert W % HEAD_DIM == 0 and W == AW and S % MOBA_BLOCK == 0
    q_col = 2 * W // HEAD_DIM
    k_col = q_col + ATTN_HEADS
    v_col = k_col + ATTN_HEADS
    gate_col = (2 * W + 3 * AW) // (D // 2)
    assert gate_col * (D // 2) == 2 * W + 3 * AW
    slopes = 2.0 ** (-8.0 * jnp.arange(1, ATTN_HEADS + 1, dtype=F32) / ATTN_HEADS)
    assert (2 * T) % ROW_TILE == 0
    n_tiles = 2 * T // ROW_TILE + N_EXPERTS
    row = lambda v: v.reshape(1, -1)

    assert depth == 1, "the final norm is fused into the single layer's combine"
    l = 0
    x2 = x.reshape(T, D)
    u = _inproj(x2, row(norm_attn_g[l]), w_in[l].astype(BF16))
    u3 = u.reshape(B, S, -1)
    y_rec = _rglru(u3, conv_w[l], row(conv_b[l]), lru_wa[l].astype(BF16), row(lru_ba[l]),
                   lru_wx[l].astype(BF16), row(lru_bx[l]), row(lru_lambda[l]), W)
    y_attn = _moba(u3, slopes, q_col, k_col, v_col)
    wr = jnp.zeros((D, LANES), F32)
    wr = wr.at[:, :N_GROUPS].set(router_group_w[l]).at[:, N_GROUPS:N_GROUPS + N_EXPERTS].set(router_expert_w[l])
    br = jnp.zeros((1, LANES), F32)
    br = br.at[0, :N_GROUPS].set(router_group_b[l]).at[0, N_GROUPS:N_GROUPS + N_EXPERTS].set(router_expert_b[l])
    x1, h2, routed = _merge(y_rec.reshape(T, W), y_attn.reshape(T, AW), u, gate_col, x2,
                            proj_rec[l].astype(BF16), proj_attn[l].astype(BF16),
                            w_out[l].astype(BF16), row(norm_ffn_g[l]), _hi_lo(wr), br)
    eid = routed[:, :2].astype(jnp.int32)
    src_tok, pos, tile_e = _sorted_layout(eid, n_tiles)
    y = _expert_ffn(h2, src_tok, tile_e, expert_w_gate[l], expert_w_up[l], expert_w_down[l])
    out = _combine(x1, routed, pos, row(norm_final_g), y)
    return out.reshape(B, S, D)
```
